```python
import math
import jax, jax.numpy as jnp
from jax import lax
import numpy as np

D_MODEL = 1024
BATCH = 8
SEQ = 2048
DEPTH = 4
DEC_BATCH = 128
DEC_SEQ = 8
PAST_LEN = 16384
PAGE_SIZE = 128

N_META = 16
N_MIXERS = 2
N_LAYERS_A = (DEPTH + 1) // 2
N_LAYERS_B = DEPTH // 2
S5_WIDTH = D_MODEL
S5_GROUP = 16
S5_GROUPS = S5_WIDTH // S5_GROUP
S5_STATE = 64
RG_WIDTH = D_MODEL
RG_BLOCKS = 4
RG_BLOCK = RG_WIDTH // RG_BLOCKS
RG_CONV = 4
RG_C = 8.0
FF_WIDTH = 2816
FF_CONV = 3
DN_ALPHA = (2 * DEPTH) ** 0.25
DN_BETA = (8 * DEPTH) ** -0.25
LN_EPS = 1e-5

kernel_name = "s5_rglru_convffn_deepnorm_meta_step"


def layer_norm(x, g, b):
    xf = x.astype(jnp.float32)
    mu = jnp.mean(xf, -1, keepdims=True)
    var = jnp.mean(jnp.square(xf - mu), -1, keepdims=True)
    y = (xf - mu) * lax.rsqrt(var + LN_EPS) * g.astype(jnp.float32) + b.astype(jnp.float32)
    return y.astype(x.dtype)


def causal_dwconv(x, hist, w, b):
    K = w.shape[0]
    L = x.shape[1]
    xp = jnp.concatenate([hist.astype(x.dtype), x], axis=1)
    y = b + w[0] * xp[:, 0:L]
    for k in range(1, K):
        y = y + w[k] * xp[:, k:k + L]
    return y, xp[:, L:]


def _complex_combine(e1, e2):
    a1r, a1i, b1r, b1i = e1
    a2r, a2i, b2r, b2i = e2
    return (a1r * a2r - a1i * a2i,
            a1r * a2i + a1i * a2r,
            a2r * b1r - a2i * b1i + b2r,
            a2r * b1i + a2i * b1r + b2i)


def _real_combine(e1, e2):
    a1, b1 = e1
    a2, b2 = e2
    return a1 * a2, a2 * b1 + b2


def s5_mixer(x, h_re, h_im, w_in, lam_re, lam_im, log_step, b_re, b_im, c_re, c_im, d_skip, w_out):
    f32 = jnp.float32
    Bsz, L, _ = x.shape
    u = x @ w_in
    ug = u.astype(f32).reshape(Bsz, L, S5_GROUPS, S5_GROUP)
    step = jnp.exp(log_step.astype(f32))[:, None]
    lr, li = lam_re.astype(f32), lam_im.astype(f32)
    mag = jnp.exp(lr * step)
    ab_re, ab_im = mag * jnp.cos(li * step), mag * jnp.sin(li * step)
    den = lr * lr + li * li
    nr, ni = ab_re - 1.0, ab_im
    q_re = ((nr * lr + ni * li) / den)[..., None]
    q_im = ((ni * lr - nr * li) / den)[..., None]
    br, bi = b_re.astype(f32), b_im.astype(f32)
    bb_re = q_re * br - q_im * bi
    bb_im = q_re * bi + q_im * br
    bu_re = jnp.einsum('blgh,gph->blgp', ug, bb_re)
    bu_im = jnp.einsum('blgh,gph->blgp', ug, bb_im)
    hr, hi = h_re.astype(f32), h_im.astype(f32)
    bu_re = bu_re.at[:, 0].add(ab_re * hr - ab_im * hi)
    bu_im = bu_im.at[:, 0].add(ab_re * hi + ab_im * hr)
    a_re = jnp.broadcast_to(ab_re, bu_re.shape)
    a_im = jnp.broadcast_to(ab_im, bu_im.shape)
    _, _, s_re, s_im = lax.associative_scan(_complex_combine, (a_re, a_im, bu_re, bu_im), axis=1)
    y = (jnp.einsum('blgp,ghp->blgh', s_re, c_re.astype(f32))
         - jnp.einsum('blgp,ghp->blgh', s_im, c_im.astype(f32)))
    y = y.reshape(Bsz, L, S5_WIDTH) + d_skip.astype(f32) * u.astype(f32)
    gy = jax.nn.gelu(y).astype(x.dtype)
    vg = gy @ w_out
    out = vg[..., :D_MODEL] * jax.nn.sigmoid(vg[..., D_MODEL:])
    return out, s_re[:, -1], s_im[:, -1]


def rglru_mixer(x, h0, conv_hist, w_in, conv_w, conv_b, w_gates, b_gates, lam, w_out):
    f32 = jnp.float32
    Bsz, L, _ = x.shape
    z = x @ w_in
    gate = jax.nn.gelu(z[..., :RG_WIDTH])
    xc, new_hist = causal_dwconv(z[..., RG_WIDTH:], conv_hist, conv_w, conv_b)
    xg = xc.reshape(Bsz, L, RG_BLOCKS, RG_BLOCK)
    gts = jnp.einsum('blnc,ncd->blnd', xg, w_gates)
    r = jax.nn.sigmoid((gts[..., :RG_BLOCK].reshape(Bsz, L, RG_WIDTH) + b_gates[:RG_WIDTH]).astype(f32))
    i = jax.nn.sigmoid((gts[..., RG_BLOCK:].reshape(Bsz, L, RG_WIDTH) + b_gates[RG_WIDTH:]).astype(f32))
    log_a = -RG_C * r * jax.nn.softplus(-lam.astype(f32))
    a = jnp.exp(log_a)
    mult = jnp.sqrt(-jnp.expm1(2.0 * log_a))
    b = mult * (i * xc.astype(f32))
    b = b.at[:, 0].add(a[:, 0] * h0.astype(f32))
    _, h = lax.associative_scan(_real_combine, (a, b), axis=1)
    y = (h.astype(x.dtype) * gate) @ w_out
    return y, h[:, -1], new_hist


def conv_ffn(x, hist, w_up, conv_w, conv_b, w_down):
    up = x @ w_up
    upc, new_hist = causal_dwconv(up, hist, conv_w, conv_b)
    y = (jax.nn.gelu(upc[..., FF_WIDTH:]) * upc[..., :FF_WIDTH]) @ w_down
    return y, new_hist


def _trunk(h, s5_re0, s5_im0, rg_h0, rg_conv0, ff_conv0, p):
    s5r, s5i, rgh, rgc, ffc = [], [], [], [], []
    for i in range(DEPTH):
        j = i // N_MIXERS
        if i % N_MIXERS == 0:
            mix, sr, si = s5_mixer(h, s5_re0[j], s5_im0[j], p['s5_w_in'][j], p['s5_lam_re'][j],
                                   p['s5_lam_im'][j], p['s5_log_step'][j], p['s5_b_re'][j],
                                   p['s5_b_im'][j], p['s5_c_re'][j], p['s5_c_im'][j],
                                   p['s5_d'][j], p['s5_w_out'][j])
            s5r.append(sr)
            s5i.append(si)
        else:
            mix, hr, cb = rglru_mixer(h, rg_h0[j], rg_conv0[j], p['rg_w_in'][j], p['rg_conv_w'][j],
                                      p['rg_conv_b'][j], p['rg_w_gates'][j], p['rg_b_gates'][j],
                                      p['rg_lam'][j], p['rg_w_out'][j])
            rgh.append(hr)
            rgc.append(cb)
        h = layer_norm(DN_ALPHA * h + mix, p['ln_g'][i, 0], p['ln_b'][i, 0])
        f, fb = conv_ffn(h, ff_conv0[i], p['ffn_w_up'][i], p['ffn_conv_w'][i], p['ffn_conv_b'][i],
                         p['ffn_w_down'][i])
        ffc.append(fb)
        h = layer_norm(DN_ALPHA * h + f, p['ln_g'][i, 1], p['ln_b'][i, 1])
    return h, jnp.stack(s5r), jnp.stack(s5i), jnp.stack(rgh), jnp.stack(rgc), jnp.stack(ffc)


def setup_inputs(seed: int = 0) -> dict:
    key = jax.random.key(seed)
    ks = list(jax.random.split(key, 40))
    nrm = lambda k, s, sc: jax.random.normal(k, s, jnp.float32) * sc
    E, G, P, H = S5_WIDTH, S5_GROUPS, S5_STATE, S5_GROUP
    R, F = RG_WIDTH, FF_WIDTH
    lam_im = jnp.pi * jnp.arange(P, dtype=jnp.float32)
    a_c = jax.random.uniform(ks[20], (N_LAYERS_B, R), jnp.float32, 0.9, 0.999)
    s = a_c ** (1.0 / RG_C)
    s5_w_out = jnp.concatenate([nrm(ks[18], (N_LAYERS_A, E, D_MODEL), E ** -0.5 * DN_BETA),
                                nrm(ks[19], (N_LAYERS_A, E, D_MODEL), E ** -0.5)], axis=-1)
    return {
        "x_prompt": nrm(ks[0], (BATCH, SEQ, D_MODEL), 1.0),
        "x_sample": nrm(ks[1], (DEC_BATCH, DEC_SEQ, D_MODEL), 1.0),
        "state_s5_re": nrm(ks[2], (N_LAYERS_A, DEC_BATCH, G, P), 0.1),
        "state_s5_im": nrm(ks[3], (N_LAYERS_A, DEC_BATCH, G, P), 0.1),
        "state_rg_h": nrm(ks[4], (N_LAYERS_B, DEC_BATCH, R), 0.5),
        "state_rg_conv": nrm(ks[5], (N_LAYERS_B, DEC_BATCH, RG_CONV - 1, R), 1.0),
        "state_ffn_conv": nrm(ks[6], (DEPTH, DEC_BATCH, FF_CONV - 1, 2 * F), 1.0),
        "meta_tokens": nrm(ks[7], (N_META, D_MODEL), 1.0),
        "s5_w_in": nrm(ks[8], (N_LAYERS_A, D_MODEL, E), D_MODEL ** -0.5),
        "s5_lam_re": -0.5 + nrm(ks[9], (N_LAYERS_A, G, P), 0.01),
        "s5_lam_im": lam_im + nrm(ks[10], (N_LAYERS_A, G, P), 0.01),
        "s5_log_step": jax.random.uniform(ks[11], (N_LAYERS_A, G), jnp.float32,
                                          math.log(0.001), math.log(0.1)),
        "s5_b_re": nrm(ks[12], (N_LAYERS_A, G, P, H), (2 * H) ** -0.5),
        "s5_b_im": nrm(ks[13], (N_LAYERS_A, G, P, H), (2 * H) ** -0.5),
        "s5_c_re": nrm(ks[14], (N_LAYERS_A, G, H, P), 0.5),
        "s5_c_im": nrm(ks[15], (N_LAYERS_A, G, H, P), 0.5),
        "s5_d": nrm(ks[16], (N_LAYERS_A, E), 1.0),
        "s5_w_out": s5_w_out,
        "rg_w_in": nrm(ks[21], (N_LAYERS_B, D_MODEL, 2 * R), D_MODEL ** -0.5),
        "rg_conv_w": nrm(ks[22], (N_LAYERS_B, RG_CONV, R), RG_CONV ** -0.5),
        "rg_conv_b": nrm(ks[23], (N_LAYERS_B, R), 0.01),
        "rg_w_gates": nrm(ks[24], (N_LAYERS_B, RG_BLOCKS, RG_BLOCK, 2 * RG_BLOCK), RG_BLOCK ** -0.5),
        "rg_b_gates": nrm(ks[25], (N_LAYERS_B, 2 * R), 0.01),
        "rg_lam": jnp.log(s) - jnp.log1p(-s),
        "rg_w_out": nrm(ks[26], (N_LAYERS_B, R, D_MODEL), R ** -0.5 * DN_BETA),
        "ffn_w_up": nrm(ks[27], (DEPTH, D_MODEL, 2 * F), D_MODEL ** -0.5),
        "ffn_conv_w": nrm(ks[28], (DEPTH, FF_CONV, 2 * F), FF_CONV ** -0.5),
        "ffn_conv_b": nrm(ks[29], (DEPTH, 2 * F), 0.01),
        "ffn_w_down": nrm(ks[30], (DEPTH, F, D_MODEL), F ** -0.5 * DN_BETA),
        "ln_g": 1.0 + nrm(ks[31], (DEPTH, 2, D_MODEL), 0.02),
        "ln_b": nrm(ks[32], (DEPTH, 2, D_MODEL), 0.02),
    }


def reference(x_prompt, x_sample, state_s5_re, state_s5_im, state_rg_h, state_rg_conv, state_ffn_conv,
              meta_tokens, s5_w_in, s5_lam_re, s5_lam_im, s5_log_step, s5_b_re, s5_b_im, s5_c_re,
              s5_c_im, s5_d, s5_w_out, rg_w_in, rg_conv_w, rg_conv_b, rg_w_gates, rg_b_gates, rg_lam,
              rg_w_out, ffn_w_up, ffn_conv_w, ffn_conv_b, ffn_w_down, ln_g, ln_b):
    p = dict(s5_w_in=s5_w_in, s5_lam_re=s5_lam_re, s5_lam_im=s5_lam_im, s5_log_step=s5_log_step,
             s5_b_re=s5_b_re, s5_b_im=s5_b_im, s5_c_re=s5_c_re, s5_c_im=s5_c_im, s5_d=s5_d,
             s5_w_out=s5_w_out, rg_w_in=rg_w_in, rg_conv_w=rg_conv_w, rg_conv_b=rg_conv_b,
             rg_w_gates=rg_w_gates, rg_b_gates=rg_b_gates, rg_lam=rg_lam, rg_w_out=rg_w_out,
             ffn_w_up=ffn_w_up, ffn_conv_w=ffn_conv_w, ffn_conv_b=ffn_conv_b, ffn_w_down=ffn_w_down,
             ln_g=ln_g, ln_b=ln_b)
    bp = x_prompt.shape[0]
    dt = x_prompt.dtype
    h_p = jnp.concatenate([jnp.broadcast_to(meta_tokens[None].astype(dt), (bp, N_META, D_MODEL)),
                           x_prompt], axis=1)
    z_s5 = jnp.zeros((N_LAYERS_A, bp, S5_GROUPS, S5_STATE), jnp.float32)
    z_rgh = jnp.zeros((N_LAYERS_B, bp, RG_WIDTH), jnp.float32)
    z_rgc = jnp.zeros((N_LAYERS_B, bp, RG_CONV - 1, RG_WIDTH), dt)
    z_ffc = jnp.zeros((DEPTH, bp, FF_CONV - 1, 2 * FF_WIDTH), dt)
    yp, s5r_p, s5i_p, rgh_p, rgc_p, ffc_p = _trunk(h_p, z_s5, z_s5, z_rgh, z_rgc, z_ffc, p)
    ys, s5r_s, s5i_s, rgh_s, rgc_s, ffc_s = _trunk(x_sample, state_s5_re, state_s5_im, state_rg_h,
                                                    state_rg_conv, state_ffn_conv, p)
    return (yp[:, N_META:], ys, s5r_p, s5i_p, rgh_p, rgc_p, ffc_p, s5r_s, s5i_s, rgh_s, rgc_s, ffc_s)
```

```python
import functools
import math

import jax
import jax.numpy as jnp
from jax import lax
from jax.experimental import pallas as pl
from jax.experimental.pallas import tpu as pltpu

N_META = 16
S5_GROUP = 16
S5_STATE = 64
RG_BLOCKS = 4
RG_C = 8.0
LN_EPS = 1e-5

SUBLANES = 8
S5_CHUNK_GROUPS = 8
FF_TILE = 256
VMEM_LIMIT_BYTES = 56 * 1024 * 1024

BF16 = jnp.bfloat16
F32 = jnp.float32


def _dot(a, b):
    return jnp.dot(a, b, preferred_element_type=F32)


def _layer_norm(z, g, b):
    mu = jnp.mean(z, axis=-1, keepdims=True)
    zc = z - mu
    var = jnp.mean(zc * zc, axis=-1, keepdims=True)
    return zc * lax.rsqrt(var + LN_EPS) * g + b


def _time_chunk(seq_len, batch, max_rows):
    best = 1
    for tc in range(1, seq_len + 1):
        if seq_len % tc == 0 and tc * batch <= max_rows:
            best = tc
    return best


def _unroll(n):
    for u in (8, 4, 2):
        if n % u == 0:
            return u
    return 1


def _s5_kernel(x_ref, hre0_ref, him0_ref, win_ref, bblk_ref, cblk_ref, are_ref, aim_ref, d_ref,
               wout_ref, g_ref, b_ref, out_ref, sre_ref, sim_ref, bu_scr, *, batch, steps, alpha):
    d_model = x_ref.shape[1]
    n_chunks = bblk_ref.shape[0]
    cin = bblk_ref.shape[1]
    cst = bblk_ref.shape[2] // 2

    @pl.when(pl.program_id(0) == 0)
    def _():
        sre_ref[...] = hre0_ref[...]
        sim_ref[...] = him0_ref[...]

    x = x_ref[...]
    u = _dot(x.astype(BF16), win_ref[...])
    ys = []
    for c in range(n_chunks):
        uc = u[:, c * cin:(c + 1) * cin].astype(BF16)
        bu_scr[...] = _dot(uc, bblk_ref[c])
        a_re = jnp.broadcast_to(are_ref[c], (SUBLANES, cst))
        a_im = jnp.broadcast_to(aim_ref[c], (SUBLANES, cst))

        def row_group(rg, carry, c=c, a_re=a_re, a_im=a_im):
            r0 = pl.multiple_of(rg * SUBLANES, SUBLANES)
            s_re = sre_ref[pl.ds(r0, SUBLANES), c * cst:(c + 1) * cst]
            s_im = sim_ref[pl.ds(r0, SUBLANES), c * cst:(c + 1) * cst]

            def step(t, s):
                s_re, s_im = s
                row = pl.multiple_of(t * batch + r0, SUBLANES)
                n_re = a_re * s_re - a_im * s_im + bu_scr[pl.ds(row, SUBLANES), 0:cst]
                n_im = a_re * s_im + a_im * s_re + bu_scr[pl.ds(row, SUBLANES), cst:2 * cst]
                bu_scr[pl.ds(row, SUBLANES), 0:cst] = n_re
                bu_scr[pl.ds(row, SUBLANES), cst:2 * cst] = n_im
                return n_re, n_im

            s_re, s_im = lax.fori_loop(0, steps, step, (s_re, s_im), unroll=_unroll(steps))
            sre_ref[pl.ds(r0, SUBLANES), c * cst:(c + 1) * cst] = s_re
            sim_ref[pl.ds(r0, SUBLANES), c * cst:(c + 1) * cst] = s_im
            return carry

        lax.fori_loop(0, batch // SUBLANES, row_group, 0)
        ys.append(_dot(bu_scr[...].astype(BF16), cblk_ref[c]))
    y = jnp.concatenate(ys, axis=1) + d_ref[...] * u
    gy = jax.nn.gelu(y).astype(BF16)
    vg = _dot(gy, wout_ref[...])
    mix = vg[:, :d_model] * jax.nn.sigmoid(vg[:, d_model:])
    out_ref[...] = _layer_norm(alpha * x + mix, g_ref[...], b_ref[...])


def _s5_layer(h, hre0, him0, p, ln_g, ln_b, *, batch, seq_len, alpha):
    rows, d_model = h.shape
    steps = _time_chunk(seq_len, batch, 768)
    m = steps * batch
    n_chunks, cin, cst2 = p["bblk"].shape
    n_state = hre0.shape[1]
    const = lambda shape: pl.BlockSpec(shape, lambda i: (0,) * len(shape))
    kern = functools.partial(_s5_kernel, batch=batch, steps=steps, alpha=alpha)
    return pl.pallas_call(
        kern,
        grid=(rows // m,),
        in_specs=[
            pl.BlockSpec((m, d_model), lambda i: (i, 0)),
            const((batch, n_state)), const((batch, n_state)),
            const(p["w_in"].shape), const(p["bblk"].shape), const(p["cblk"].shape),
            const(p["a_re"].shape), const(p["a_im"].shape), const((1, d_model)),
            const(p["w_out"].shape), const((1, d_model)), const((1, d_model)),
        ],
        out_specs=[
            pl.BlockSpec((m, d_model), lambda i: (i, 0)),
            const((batch, n_state)), const((batch, n_state)),
        ],
        out_shape=[
            jax.ShapeDtypeStruct((rows, d_model), F32),
            jax.ShapeDtypeStruct((batch, n_state), F32),
            jax.ShapeDtypeStruct((batch, n_state), F32),
        ],
        scratch_shapes=[pltpu.VMEM((m, cst2), F32)],
        compiler_params=pltpu.CompilerParams(
            dimension_semantics=("arbitrary",), vmem_limit_bytes=VMEM_LIMIT_BYTES),
        name="s5_mixer_ln",
    )(h, hre0, him0, p["w_in"], p["bblk"], p["cblk"], p["a_re"], p["a_im"], p["d"], p["w_out"],
      ln_g, ln_b)


def _s5_params(w_in, lam_re, lam_im, log_step, b_re, b_im, c_re, c_im, d_skip, w_out):
    n_groups, n_state, gsz = b_re.shape
    gc = S5_CHUNK_GROUPS
    nc = n_groups // gc
    step = jnp.exp(log_step.astype(F32))[:, None]
    lr, li = lam_re.astype(F32), lam_im.astype(F32)
    mag = jnp.exp(lr * step)
    ab_re, ab_im = mag * jnp.cos(li * step), mag * jnp.sin(li * step)
    den = lr * lr + li * li
    nr, ni = ab_re - 1.0, ab_im
    q_re = ((nr * lr + ni * li) / den)[..., None]
    q_im = ((ni * lr - nr * li) / den)[..., None]
    br, bi = b_re.astype(F32), b_im.astype(F32)
    bb_re = q_re * br - q_im * bi
    bb_im = q_re * bi + q_im * br
    eye = jnp.eye(gc, dtype=F32)

    def b_block(bb):
        t = bb.reshape(nc, gc, 1, n_state, gsz) * eye[None, :, :, None, None]
        return t.transpose(0, 1, 4, 2, 3).reshape(nc, gc * gsz, gc * n_state)

    def c_block(cc):
        t = cc.astype(F32).reshape(nc, gc, 1, gsz, n_state) * eye[None, :, :, None, None]
        return t.transpose(0, 1, 4, 2, 3).reshape(nc, gc * n_state, gc * gsz)

    bblk = jnp.concatenate([b_block(bb_re), b_block(bb_im)], axis=2).astype(BF16)
    cblk = jnp.concatenate([c_block(c_re), -c_block(c_im)], axis=1).astype(BF16)
    return dict(
        w_in=w_in.astype(BF16), w_out=w_out.astype(BF16), bblk=bblk, cblk=cblk,
        a_re=ab_re.reshape(nc, 1, gc * n_state), a_im=ab_im.reshape(nc, 1, gc * n_state),
        d=d_skip.astype(F32).reshape(1, -1))


def _softplus(z):
    return jnp.maximum(z, 0.0) + jnp.log1p(jnp.exp(-jnp.abs(z)))


def _rg_kernel(x_ref, h0_ref, hist0_ref, win_ref, cw_ref, cb_ref, wg_ref, bg_ref, lam_ref, wout_ref,
               g_ref, b_ref, out_ref, hn_ref, histn_ref, ext_scr, a_scr, b_scr, *, batch, steps, alpha):
    m, d_model = x_ref.shape
    width = a_scr.shape[1]
    n_taps = cw_ref.shape[0]
    n_hist = (n_taps - 1) * batch
    blk = width // RG_BLOCKS

    @pl.when(pl.program_id(0) == 0)
    def _():
        hn_ref[...] = h0_ref[...]
        histn_ref[...] = hist0_ref[...]

    x = x_ref[...]
    z = _dot(x.astype(BF16), win_ref[...])
    gate = jax.nn.gelu(z[:, :width])
    ext_scr[0:n_hist, :] = histn_ref[...]
    ext_scr[n_hist:n_hist + m, :] = z[:, width:]
    histn_ref[...] = ext_scr[m:m + n_hist, :]
    xc = cb_ref[...] + cw_ref[0:1, :] * ext_scr[0:m, :]
    for k in range(1, n_taps):
        xc = xc + cw_ref[k:k + 1, :] * ext_scr[k * batch:k * batch + m, :]
    rs, is_ = [], []
    for n in range(RG_BLOCKS):
        gts = _dot(xc[:, n * blk:(n + 1) * blk].astype(BF16), wg_ref[n])
        rs.append(gts[:, :blk])
        is_.append(gts[:, blk:])
    r = jax.nn.sigmoid(jnp.concatenate(rs, axis=1) + bg_ref[:, :width])
    ig = jax.nn.sigmoid(jnp.concatenate(is_, axis=1) + bg_ref[:, width:])
    log_a = -RG_C * r * _softplus(-lam_ref[...])
    a_scr[...] = jnp.exp(log_a)
    th = jnp.tanh(log_a)
    b_scr[...] = jnp.sqrt(-2.0 * th / (1.0 - th)) * (ig * xc)

    def row_group(rg, carry):
        r0 = pl.multiple_of(rg * SUBLANES, SUBLANES)

        def step(t, h):
            row = pl.multiple_of(t * batch + r0, SUBLANES)
            h = a_scr[pl.ds(row, SUBLANES), :] * h + b_scr[pl.ds(row, SUBLANES), :]
            b_scr[pl.ds(row, SUBLANES), :] = h
            return h

        h = lax.fori_loop(0, steps, step, hn_ref[pl.ds(r0, SUBLANES), :], unroll=_unroll(steps))
        hn_ref[pl.ds(r0, SUBLANES), :] = h
        return carry

    lax.fori_loop(0, batch // SUBLANES, row_group, 0)
    y = _dot((b_scr[...] * gate).astype(BF16), wout_ref[...])
    out_ref[...] = _layer_norm(alpha * x + y, g_ref[...], b_ref[...])


def _rg_layer(h, h0, hist0, p, ln_g, ln_b, *, batch, seq_len, alpha):
    rows, d_model = h.shape
    width = h0.shape[1]
    n_hist = hist0.shape[0]
    steps = _time_chunk(seq_len, batch, 768)
    m = steps * batch
    const = lambda shape: pl.BlockSpec(shape, lambda i: (0,) * len(shape))
    kern = functools.partial(_rg_kernel, batch=batch, steps=steps, alpha=alpha)
    return pl.pallas_call(
        kern,
        grid=(rows // m,),
        in_specs=[
            pl.BlockSpec((m, d_model), lambda i: (i, 0)),
            const((batch, width)), const((n_hist, width)),
            const(p["w_in"].shape), const(p["conv_w"].shape), const((1, width)),
            const(p["w_gates"].shape), const((1, 2 * width)), const((1, width)),
            const(p["w_out"].shape), const((1, d_model)), const((1, d_model)),
        ],
        out_specs=[
            pl.BlockSpec((m, d_model), lambda i: (i, 0)),
            const((batch, width)), const((n_hist, width)),
        ],
        out_shape=[
            jax.ShapeDtypeStruct((rows, d_model), F32),
            jax.ShapeDtypeStruct((batch, width), F32),
            jax.ShapeDtypeStruct((n_hist, width), F32),
        ],
        scratch_shapes=[
            pltpu.VMEM((m + n_hist, width), F32),
            pltpu.VMEM((m, width), F32),
            pltpu.VMEM((m, width), F32),
        ],
        compiler_params=pltpu.CompilerParams(
            dimension_semantics=("arbitrary",), vmem_limit_bytes=VMEM_LIMIT_BYTES),
        name="rglru_mixer_ln",
    )(h, h0, hist0, p["w_in"], p["conv_w"], p["conv_b"], p["w_gates"], p["b_gates"], p["lam"],
      p["w_out"], ln_g, ln_b)


def _ffn_kernel(x_ref, hist0_ref, wv_ref, wg_ref, cwv_ref, cwg_ref, cbv_ref, cbg_ref, wd_ref,
                g_ref, b_ref, out_ref, histn_ref, xb_scr, acc_scr, ext_scr, *, batch, alpha):
    m = x_ref.shape[0]
    n_taps = cwv_ref.shape[0]
    n_hist = (n_taps - 1) * batch
    i = pl.program_id(0)
    j = pl.program_id(1)

    @pl.when(j == 0)
    def _():
        xb_scr[...] = x_ref[...].astype(BF16)
        acc_scr[...] = jnp.zeros_like(acc_scr)

    @pl.when(i == 0)
    def _():
        histn_ref[j] = hist0_ref[0]

    xb = xb_scr[...]

    def conv(part, up, cw_ref, cb_ref):
        ext_scr[part, 0:n_hist, :] = histn_ref[j, part]
        ext_scr[part, n_hist:n_hist + m, :] = up
        histn_ref[j, part] = ext_scr[part, m:m + n_hist, :]
        y = cb_ref[...] + cw_ref[0:1, :] * ext_scr[part, 0:m, :]
        for k in range(1, n_taps):
            y = y + cw_ref[k:k + 1, :] * ext_scr[part, k * batch:k * batch + m, :]
        return y

    val = conv(0, _dot(xb, wv_ref[...]), cwv_ref, cbv_ref)
    gate = conv(1, _dot(xb, wg_ref[...]), cwg_ref, cbg_ref)
    act = (jax.nn.gelu(gate) * val).astype(BF16)
    acc_scr[...] += _dot(act, wd_ref[...])

    @pl.when(j == pl.num_programs(1) - 1)
    def _():
        out_ref[...] = _layer_norm(alpha * x_ref[...] + acc_scr[...], g_ref[...], b_ref[...])


def _ffn_layer(h, hist0, p, ln_g, ln_b, *, batch, seq_len, alpha):
    rows, d_model = h.shape
    n_tiles, _, n_hist, tile = hist0.shape
    steps = _time_chunk(seq_len, batch, 1024)
    m = steps * batch
    kern = functools.partial(_ffn_kernel, batch=batch, alpha=alpha)
    n_taps = p["conv_w"].shape[0]
    return pl.pallas_call(
        kern,
        grid=(rows // m, n_tiles),
        in_specs=[
            pl.BlockSpec((m, d_model), lambda i, j: (i, 0)),
            pl.BlockSpec((1, 2, n_hist, tile), lambda i, j: (j, 0, 0, 0)),
            pl.BlockSpec((d_model, tile), lambda i, j: (0, j)),
            pl.BlockSpec((d_model, tile), lambda i, j: (0, n_tiles + j)),
            pl.BlockSpec((n_taps, tile), lambda i, j: (0, j)),
            pl.BlockSpec((n_taps, tile), lambda i, j: (0, n_tiles + j)),
            pl.BlockSpec((1, tile), lambda i, j: (0, j)),
            pl.BlockSpec((1, tile), lambda i, j: (0, n_tiles + j)),
            pl.BlockSpec((tile, d_model), lambda i, j: (j, 0)),
            pl.BlockSpec((1, d_model), lambda i, j: (0, 0)),
            pl.BlockSpec((1, d_model), lambda i, j: (0, 0)),
        ],
        out_specs=[
            pl.BlockSpec((m, d_model), lambda i, j: (i, 0)),
            pl.BlockSpec((n_tiles, 2, n_hist, tile), lambda i, j: (0, 0, 0, 0)),
        ],
        out_shape=[
            jax.ShapeDtypeStruct((rows, d_model), F32),
            jax.ShapeDtypeStruct((n_tiles, 2, n_hist, tile), F32),
        ],
        scratch_shapes=[
            pltpu.VMEM((m, d_model), BF16),
            pltpu.VMEM((m, d_model), F32),
            pltpu.VMEM((2, m + n_hist, tile), F32),
        ],
        compiler_params=pltpu.CompilerParams(
            dimension_semantics=("arbitrary", "arbitrary"), vmem_limit_bytes=VMEM_LIMIT_BYTES),
        name="convffn_ln",
    )(h, hist0, p["w_up"], p["w_up"], p["conv_w"], p["conv_w"], p["conv_b"], p["conv_b"],
      p["w_down"], ln_g, ln_b)


def _ffn_hist_to_tiles(hist, n_tiles):
    b, k, _ = hist.shape
    t = hist.reshape(b, k, 2, n_tiles, FF_TILE).transpose(3, 2, 1, 0, 4)
    return t.reshape(n_tiles, 2, k * b, FF_TILE)


def _ffn_hist_from_tiles(tiles, batch):
    n_tiles, _, n_hist, tile = tiles.shape
    k = n_hist // batch
    t = tiles.reshape(n_tiles, 2, k, batch, tile).transpose(3, 2, 1, 0, 4)
    return t.reshape(batch, k, 2 * n_tiles * tile)


def _trunk(h, s5_re0, s5_im0, rg_h0, rg_conv0, ff_conv0, params, *, batch, seq_len, alpha):
    depth = len(params["ffn"])
    s5r, s5i, rgh, rgc, ffc = [], [], [], [], []
    for i in range(depth):
        j = i // 2
        ln_g, ln_b = params["ln_g"][i], params["ln_b"][i]
        if i % 2 == 0:
            n_state = s5_re0.shape[2] * s5_re0.shape[3]
            h, sr, si = _s5_layer(h, s5_re0[j].reshape(batch, n_state), s5_im0[j].reshape(batch, n_state),
                                  params["s5"][j], ln_g[0:1], ln_b[0:1],
                                  batch=batch, seq_len=seq_len, alpha=alpha)
            s5r.append(sr.reshape(s5_re0.shape[1:]))
            s5i.append(si.reshape(s5_re0.shape[1:]))
        else:
            k1 = rg_conv0.shape[2]
            hist0 = rg_conv0[j].transpose(1, 0, 2).reshape(k1 * batch, -1)
            h, hn, histn = _rg_layer(h, rg_h0[j], hist0, params["rg"][j], ln_g[0:1], ln_b[0:1],
                                     batch=batch, seq_len=seq_len, alpha=alpha)
            rgh.append(hn)
            rgc.append(histn.reshape(k1, batch, -1).transpose(1, 0, 2))
        n_tiles = ff_conv0.shape[3] // (2 * FF_TILE)
        h, histn = _ffn_layer(h, _ffn_hist_to_tiles(ff_conv0[i], n_tiles), params["ffn"][i],
                              ln_g[1:2], ln_b[1:2], batch=batch, seq_len=seq_len, alpha=alpha)
        ffc.append(_ffn_hist_from_tiles(histn, batch))
    return h, jnp.stack(s5r), jnp.stack(s5i), jnp.stack(rgh), jnp.stack(rgc), jnp.stack(ffc)


def kernel(x_prompt, x_sample, state_s5_re, state_s5_im, state_rg_h, state_rg_conv, state_ffn_conv,
           meta_tokens, s5_w_in, s5_lam_re, s5_lam_im, s5_log_step, s5_b_re, s5_b_im, s5_c_re,
           s5_c_im, s5_d, s5_w_out, rg_w_in, rg_conv_w, rg_conv_b, rg_w_gates, rg_b_gates, rg_lam,
           rg_w_out, ffn_w_up, ffn_conv_w, ffn_conv_b, ffn_w_down, ln_g, ln_b):
    depth = ffn_w_up.shape[0]
    alpha = (2 * depth) ** 0.25
    bp, seq, d_model = x_prompt.shape
    bs, dec_seq, _ = x_sample.shape
    dt = x_prompt.dtype

    params = dict(
        s5=[_s5_params(s5_w_in[j], s5_lam_re[j], s5_lam_im[j], s5_log_step[j], s5_b_re[j], s5_b_im[j],
                       s5_c_re[j], s5_c_im[j], s5_d[j], s5_w_out[j]) for j in range(s5_w_in.shape[0])],
        rg=[dict(w_in=rg_w_in[j].astype(BF16), conv_w=rg_conv_w[j], conv_b=rg_conv_b[j].reshape(1, -1),
                 w_gates=rg_w_gates[j].astype(BF16), b_gates=rg_b_gates[j].reshape(1, -1),
                 lam=rg_lam[j].astype(F32).reshape(1, -1), w_out=rg_w_out[j].astype(BF16))
            for j in range(rg_w_in.shape[0])],
        ffn=[dict(w_up=ffn_w_up[i].astype(BF16), conv_w=ffn_conv_w[i], conv_b=ffn_conv_b[i].reshape(1, -1),
                  w_down=ffn_w_down[i].astype(BF16)) for i in range(depth)],
        ln_g=ln_g, ln_b=ln_b)

    lp = N_META + seq
    h_p = jnp.concatenate([jnp.broadcast_to(meta_tokens[:, None].astype(dt), (N_META, bp, d_model)),
                           x_prompt.transpose(1, 0, 2)], axis=0).reshape(lp * bp, d_model)
    zeros_like_b = lambda s: jnp.zeros((s.shape[0], bp) + s.shape[2:], F32)
    yp, s5r_p, s5i_p, rgh_p, rgc_p, ffc_p = _trunk(
        h_p, zeros_like_b(state_s5_re), zeros_like_b(state_s5_im), zeros_like_b(state_rg_h),
        zeros_like_b(state_rg_conv), zeros_like_b(state_ffn_conv), params,
        batch=bp, seq_len=lp, alpha=alpha)
    y_prompt = yp.reshape(lp, bp, d_model)[N_META:].transpose(1, 0, 2)

    h_s = x_sample.transpose(1, 0, 2).reshape(dec_seq * bs, d_model)
    ys, s5r_s, s5i_s, rgh_s, rgc_s, ffc_s = _trunk(
        h_s, state_s5_re, state_s5_im, state_rg_h, state_rg_conv, state_ffn_conv, params,
        batch=bs, seq_len=dec_seq, alpha=alpha)
    y_sample = ys.reshape(dec_seq, bs, d_model).transpose(1, 0, 2)

    return (y_prompt, y_sample, s5r_p, s5i_p, rgh_p, rgc_p, ffc_p, s5r_s, s5i_s, rgh_s, rgc_s, ffc_s)
```

```python
import functools

import jax
import jax.numpy as jnp
from jax import lax
from jax.experimental import pallas as pl
from jax.experimental.pallas import tpu as pltpu

RG_BLOCKS = 4
RG_C = 8.0
LN_EPS = 1e-5

SUBLANES = 8
S5_CHUNK_GROUPS = 8
FF_TILE = 256
BLOCK_ROWS = 1024
SUB_ROWS = 512
VMEM_LIMIT_BYTES = 56 * 1024 * 1024

BF16 = jnp.bfloat16
F32 = jnp.float32


def _dot(a, b):
    return jnp.dot(a, b, preferred_element_type=F32)


def _layer_norm(z, g, b):
    mu = jnp.mean(z, axis=-1, keepdims=True)
    zc = z - mu
    var = jnp.mean(zc * zc, axis=-1, keepdims=True)
    return zc * lax.rsqrt(var + LN_EPS) * g + b


def _largest_divisor(n, limit):
    return max(d for d in range(1, n + 1) if n % d == 0 and d <= limit)


def _tiling(seq_len, batch, min_steps):
    steps = _largest_divisor(seq_len, max(BLOCK_ROWS // batch, min_steps))
    sub = _largest_divisor(steps, max(SUB_ROWS // batch, min_steps))
    assert sub >= min_steps, (seq_len, batch, min_steps)
    return steps, sub


def _const_spec(shape):
    return pl.BlockSpec(shape, lambda i: (0,) * len(shape), pipeline_mode=pl.Buffered(1))


def _slab_scan(step_fn, carry, n_steps, batch):
    n_groups = batch // SUBLANES
    outs = [[None] * n_groups for _ in range(n_steps)]
    carry = list(carry)
    for t in range(n_steps):
        for rg in range(n_groups):
            row = t * batch + rg * SUBLANES
            carry[rg], outs[t][rg] = step_fn(carry[rg], row)
    n_out = len(outs[0][0])
    stacked = [jnp.concatenate([outs[t][rg][k] for t in range(n_steps) for rg in range(n_groups)], axis=0)
               for k in range(n_out)]
    return carry, stacked


def _s5_kernel(*refs, batch, sub_steps, alpha, zero_init):
    if zero_init:
        (x_ref, win_ref, bblk_ref, cblk_ref, are_ref, aim_ref, d_ref, wout_ref, g_ref, b_ref,
         out_ref, sre_ref, sim_ref) = refs
    else:
        (x_ref, hre0_ref, him0_ref, win_ref, bblk_ref, cblk_ref, are_ref, aim_ref, d_ref, wout_ref,
         g_ref, b_ref, out_ref, sre_ref, sim_ref) = refs
    m, d_model = x_ref.shape
    n_chunks, cin, cst2 = bblk_ref.shape
    cst = cst2 // 2
    sub_rows = sub_steps * batch
    n_groups = batch // SUBLANES

    @pl.when(pl.program_id(0) == 0)
    def _():
        if zero_init:
            sre_ref[...] = jnp.zeros_like(sre_ref)
            sim_ref[...] = jnp.zeros_like(sim_ref)
        else:
            sre_ref[...] = hre0_ref[...]
            sim_ref[...] = him0_ref[...]

    def rows_of(ref, rg, c):
        return ref[rg * SUBLANES:(rg + 1) * SUBLANES, c * cst:(c + 1) * cst]

    state = [[(rows_of(sre_ref, rg, c), rows_of(sim_ref, rg, c)) for rg in range(n_groups)]
             for c in range(n_chunks)]
    for s in range(m // sub_rows):
        r0 = s * sub_rows
        x = x_ref[r0:r0 + sub_rows, :]
        u = _dot(x.astype(BF16), win_ref[...])
        ys = []
        for c in range(n_chunks):
            bu = _dot(u[:, c * cin:(c + 1) * cin].astype(BF16), bblk_ref[c])
            a_re = jnp.broadcast_to(are_ref[c], (SUBLANES, cst))
            a_im = jnp.broadcast_to(aim_ref[c], (SUBLANES, cst))

            def step(carry, row, bu=bu, a_re=a_re, a_im=a_im):
                s_re, s_im = carry
                n_re = a_re * s_re - a_im * s_im + bu[row:row + SUBLANES, 0:cst]
                n_im = a_re * s_im + a_im * s_re + bu[row:row + SUBLANES, cst:cst2]
                return (n_re, n_im), (n_re, n_im)

            state[c], (st_re, st_im) = _slab_scan(step, state[c], sub_steps, batch)
            st = jnp.concatenate([st_re, st_im], axis=1).astype(BF16)
            ys.append(_dot(st, cblk_ref[c]))
        y = jnp.concatenate(ys, axis=1) + d_ref[...] * u
        vg = _dot(jax.nn.gelu(y).astype(BF16), wout_ref[...])
        mix = vg[:, :d_model] * jax.nn.sigmoid(vg[:, d_model:])
        out_ref[r0:r0 + sub_rows, :] = _layer_norm(alpha * x + mix, g_ref[...], b_ref[...])
    for c in range(n_chunks):
        for rg in range(n_groups):
            sre_ref[rg * SUBLANES:(rg + 1) * SUBLANES, c * cst:(c + 1) * cst] = state[c][rg][0]
            sim_ref[rg * SUBLANES:(rg + 1) * SUBLANES, c * cst:(c + 1) * cst] = state[c][rg][1]


def _s5_layer(h, init, p, ln_g, ln_b, *, batch, seq_len, alpha):
    rows, d_model = h.shape
    steps, sub_steps = _tiling(seq_len, batch, 1)
    m = steps * batch
    n_state = p["a_re"].shape[0] * p["a_re"].shape[2]
    zero_init = init is None
    state_spec = _const_spec((batch, n_state))
    weights = [p["w_in"], p["bblk"], p["cblk"], p["a_re"], p["a_im"], p["d"], p["w_out"], ln_g, ln_b]
    kern = functools.partial(_s5_kernel, batch=batch, sub_steps=sub_steps, alpha=alpha,
                             zero_init=zero_init)
    return pl.pallas_call(
        kern,
        grid=(rows // m,),
        in_specs=([pl.BlockSpec((m, d_model), lambda i: (i, 0))]
                  + ([] if zero_init else [state_spec, state_spec])
                  + [_const_spec(w.shape) for w in weights]),
        out_specs=[
            pl.BlockSpec((m, d_model), lambda i: (i, 0)),
            pl.BlockSpec((batch, n_state), lambda i: (0, 0)),
            pl.BlockSpec((batch, n_state), lambda i: (0, 0)),
        ],
        out_shape=[
            jax.ShapeDtypeStruct((rows, d_model), F32),
            jax.ShapeDtypeStruct((batch, n_state), F32),
            jax.ShapeDtypeStruct((batch, n_state), F32),
        ],
        compiler_params=pltpu.CompilerParams(
            dimension_semantics=("arbitrary",), vmem_limit_bytes=VMEM_LIMIT_BYTES),
        name="s5_mixer_ln",
    )(h, *([] if zero_init else list(init)), *weights)


def _s5_params(w_in, lam_re, lam_im, log_step, b_re, b_im, c_re, c_im, d_skip, w_out):
    n_groups, n_state, gsz = b_re.shape
    gc = S5_CHUNK_GROUPS
    nc = n_groups // gc
    step = jnp.exp(log_step.astype(F32))[:, None]
    lr, li = lam_re.astype(F32), lam_im.astype(F32)
    mag = jnp.exp(lr * step)
    ab_re, ab_im = mag * jnp.cos(li * step), mag * jnp.sin(li * step)
    den = lr * lr + li * li
    nr, ni = ab_re - 1.0, ab_im
    q_re = ((nr * lr + ni * li) / den)[..., None]
    q_im = ((ni * lr - nr * li) / den)[..., None]
    br, bi = b_re.astype(F32), b_im.astype(F32)
    bb_re = q_re * br - q_im * bi
    bb_im = q_re * bi + q_im * br
    eye = jnp.eye(gc, dtype=F32)

    def b_block(bb):
        t = bb.reshape(nc, gc, 1, n_state, gsz) * eye[None, :, :, None, None]
        return t.transpose(0, 1, 4, 2, 3).reshape(nc, gc * gsz, gc * n_state)

    def c_block(cc):
        t = cc.astype(F32).reshape(nc, gc, 1, gsz, n_state) * eye[None, :, :, None, None]
        return t.transpose(0, 1, 4, 2, 3).reshape(nc, gc * n_state, gc * gsz)

    bblk = jnp.concatenate([b_block(bb_re), b_block(bb_im)], axis=2).astype(BF16)
    cblk = jnp.concatenate([c_block(c_re), -c_block(c_im)], axis=1).astype(BF16)
    return dict(
        w_in=w_in.astype(BF16), w_out=w_out.astype(BF16), bblk=bblk, cblk=cblk,
        a_re=ab_re.reshape(nc, 1, gc * n_state), a_im=ab_im.reshape(nc, 1, gc * n_state),
        d=d_skip.astype(F32).reshape(1, -1))


def _softplus(z):
    return jnp.maximum(z, 0.0) + jnp.log1p(jnp.exp(-jnp.abs(z)))


def _causal_conv(up, hist_ref, col, width, cw_ref, cb_ref, batch):
    n_taps = cw_ref.shape[0]
    n_ch = hist_ref.shape[1] // (n_taps - 1)
    rows = up.shape[0]
    hist = [hist_ref[:, k * n_ch + col:k * n_ch + col + width] for k in range(n_taps - 1)]
    ext = jnp.concatenate(hist + [up], axis=0)
    for k in range(n_taps - 1):
        hist_ref[:, k * n_ch + col:k * n_ch + col + width] = ext[rows + k * batch:rows + (k + 1) * batch]
    y = cb_ref[:, col:col + width] + cw_ref[0:1, col:col + width] * ext[0:rows]
    for k in range(1, n_taps):
        y = y + cw_ref[k:k + 1, col:col + width] * ext[k * batch:k * batch + rows]
    return y


def _rg_kernel(*refs, batch, sub_steps, alpha, zero_init):
    if zero_init:
        (x_ref, win_ref, cw_ref, cb_ref, wg_ref, bg_ref, lam_ref, wout_ref, g_ref, b_ref,
         out_ref, hn_ref, histn_ref) = refs
    else:
        (x_ref, h0_ref, hist0_ref, win_ref, cw_ref, cb_ref, wg_ref, bg_ref, lam_ref, wout_ref,
         g_ref, b_ref, out_ref, hn_ref, histn_ref) = refs
    m, d_model = x_ref.shape
    width = hn_ref.shape[1]
    blk = width // RG_BLOCKS
    sub_rows = sub_steps * batch
    n_groups = batch // SUBLANES

    @pl.when(pl.program_id(0) == 0)
    def _():
        if zero_init:
            hn_ref[...] = jnp.zeros_like(hn_ref)
            histn_ref[...] = jnp.zeros_like(histn_ref)
        else:
            hn_ref[...] = h0_ref[...]
            histn_ref[...] = hist0_ref[...]

    neg_c_softplus = -RG_C * _softplus(-lam_ref[...])
    h_state = [hn_ref[rg * SUBLANES:(rg + 1) * SUBLANES, :] for rg in range(n_groups)]
    for s in range(m // sub_rows):
        r0 = s * sub_rows
        x = x_ref[r0:r0 + sub_rows, :]
        z = _dot(x.astype(BF16), win_ref[...])
        gate = jax.nn.gelu(z[:, :width])
        xc = _causal_conv(z[:, width:], histn_ref, 0, width, cw_ref, cb_ref, batch)
        rs, is_ = [], []
        for n in range(RG_BLOCKS):
            gts = _dot(xc[:, n * blk:(n + 1) * blk].astype(BF16), wg_ref[n])
            rs.append(gts[:, :blk])
            is_.append(gts[:, blk:])
        r = jax.nn.sigmoid(jnp.concatenate(rs, axis=1) + bg_ref[:, :width])
        ig = jax.nn.sigmoid(jnp.concatenate(is_, axis=1) + bg_ref[:, width:])
        log_a = r * neg_c_softplus
        a = jnp.exp(log_a)
        th = jnp.tanh(log_a)
        bb = jnp.sqrt(-2.0 * th / (1.0 - th)) * (ig * xc)

        def step(h, row, a=a, bb=bb):
            h = a[row:row + SUBLANES, :] * h + bb[row:row + SUBLANES, :]
            return h, (h,)

        h_state, (hs,) = _slab_scan(step, h_state, sub_steps, batch)
        y = _dot((hs * gate).astype(BF16), wout_ref[...])
        out_ref[r0:r0 + sub_rows, :] = _layer_norm(alpha * x + y, g_ref[...], b_ref[...])
    for rg in range(n_groups):
        hn_ref[rg * SUBLANES:(rg + 1) * SUBLANES, :] = h_state[rg]


def _rg_layer(h, init, p, ln_g, ln_b, *, batch, seq_len, alpha):
    rows, d_model = h.shape
    width = p["w_out"].shape[0]
    n_taps = p["conv_w"].shape[0]
    steps, sub_steps = _tiling(seq_len, batch, n_taps - 1)
    m = steps * batch
    zero_init = init is None
    hist_cols = (n_taps - 1) * width
    weights = [p["w_in"], p["conv_w"], p["conv_b"], p["w_gates"], p["b_gates"], p["lam"], p["w_out"],
               ln_g, ln_b]
    kern = functools.partial(_rg_kernel, batch=batch, sub_steps=sub_steps, alpha=alpha,
                             zero_init=zero_init)
    return pl.pallas_call(
        kern,
        grid=(rows // m,),
        in_specs=([pl.BlockSpec((m, d_model), lambda i: (i, 0))]
                  + ([] if zero_init else [_const_spec((batch, width)), _const_spec((batch, hist_cols))])
                  + [_const_spec(w.shape) for w in weights]),
        out_specs=[
            pl.BlockSpec((m, d_model), lambda i: (i, 0)),
            pl.BlockSpec((batch, width), lambda i: (0, 0)),
            pl.BlockSpec((batch, hist_cols), lambda i: (0, 0)),
        ],
        out_shape=[
            jax.ShapeDtypeStruct((rows, d_model), F32),
            jax.ShapeDtypeStruct((batch, width), F32),
            jax.ShapeDtypeStruct((batch, hist_cols), F32),
        ],
        compiler_params=pltpu.CompilerParams(
            dimension_semantics=("arbitrary",), vmem_limit_bytes=VMEM_LIMIT_BYTES),
        name="rglru_mixer_ln",
    )(h, *([] if zero_init else list(init)), *weights)


def _ffn_kernel(*refs, batch, sub_steps, alpha, zero_init):
    if zero_init:
        x_ref, wup_ref, cw_ref, cb_ref, wd_ref, g_ref, b_ref, out_ref, histn_ref, act_scr = refs
    else:
        (x_ref, hist0_ref, wup_ref, cw_ref, cb_ref, wd_ref, g_ref, b_ref, out_ref, histn_ref,
         act_scr) = refs
    m = x_ref.shape[0]
    ff = wd_ref.shape[0]
    sub_rows = sub_steps * batch

    @pl.when(pl.program_id(0) == 0)
    def _():
        if zero_init:
            histn_ref[...] = jnp.zeros_like(histn_ref)
        else:
            histn_ref[...] = hist0_ref[...]

    for s in range(m // sub_rows):
        r0 = s * sub_rows
        x = x_ref[r0:r0 + sub_rows, :]
        xb = x.astype(BF16)
        for j in range(ff // FF_TILE):
            val, gate = [
                _causal_conv(_dot(xb, wup_ref[:, col:col + FF_TILE]), histn_ref, col, FF_TILE,
                             cw_ref, cb_ref, batch)
                for col in (j * FF_TILE, ff + j * FF_TILE)]
            act_scr[s % 2, :, j * FF_TILE:(j + 1) * FF_TILE] = (jax.nn.gelu(gate) * val).astype(BF16)
        y = _dot(act_scr[s % 2], wd_ref[...])
        out_ref[r0:r0 + sub_rows, :] = _layer_norm(alpha * x + y, g_ref[...], b_ref[...])


def _ffn_layer(h, init, p, ln_g, ln_b, *, batch, seq_len, alpha):
    rows, d_model = h.shape
    ff = p["w_down"].shape[0]
    n_taps = p["conv_w"].shape[0]
    steps, sub_steps = _tiling(seq_len, batch, n_taps - 1)
    m = steps * batch
    zero_init = init is None
    hist_cols = (n_taps - 1) * 2 * ff
    weights = [p["w_up"], p["conv_w"], p["conv_b"], p["w_down"], ln_g, ln_b]
    kern = functools.partial(_ffn_kernel, batch=batch, sub_steps=sub_steps, alpha=alpha,
                             zero_init=zero_init)
    return pl.pallas_call(
        kern,
        grid=(rows // m,),
        in_specs=([pl.BlockSpec((m, d_model), lambda i: (i, 0))]
                  + ([] if zero_init else [_const_spec((batch, hist_cols))])
                  + [_const_spec(w.shape) for w in weights]),
        out_specs=[
            pl.BlockSpec((m, d_model), lambda i: (i, 0)),
            pl.BlockSpec((batch, hist_cols), lambda i: (0, 0)),
        ],
        out_shape=[
            jax.ShapeDtypeStruct((rows, d_model), F32),
            jax.ShapeDtypeStruct((batch, hist_cols), F32),
        ],
        scratch_shapes=[pltpu.VMEM((2, sub_steps * batch, ff), BF16)],
        compiler_params=pltpu.CompilerParams(
            dimension_semantics=("arbitrary",), vmem_limit_bytes=VMEM_LIMIT_BYTES),
        name="convffn_ln",
    )(h, *([] if zero_init else [init]), *weights)


def _trunk(h, init, params, *, batch, seq_len, alpha):
    depth = len(params["ffn"])
    new = dict(s5_re=[], s5_im=[], rg_h=[], rg_conv=[], ffn_conv=[])
    for i in range(depth):
        j = i // 2
        ln_g, ln_b = params["ln_g"][i], params["ln_b"][i]
        if i % 2 == 0:
            st = None if init is None else (init["s5_re"][j], init["s5_im"][j])
            h, sr, si = _s5_layer(h, st, params["s5"][j], ln_g[0:1], ln_b[0:1],
                                  batch=batch, seq_len=seq_len, alpha=alpha)
            new["s5_re"].append(sr)
            new["s5_im"].append(si)
        else:
            st = None if init is None else (init["rg_h"][j], init["rg_conv"][j])
            h, hn, histn = _rg_layer(h, st, params["rg"][j], ln_g[0:1], ln_b[0:1],
                                     batch=batch, seq_len=seq_len, alpha=alpha)
            new["rg_h"].append(hn)
            new["rg_conv"].append(histn)
        st = None if init is None else init["ffn_conv"][i]
        h, histn = _ffn_layer(h, st, params["ffn"][i], ln_g[1:2], ln_b[1:2],
                              batch=batch, seq_len=seq_len, alpha=alpha)
        new["ffn_conv"].append(histn)
    return h, new


def _flat_states(s5_re, s5_im, rg_h, rg_conv, ffn_conv):
    flat = lambda a: [a[k].reshape(a.shape[1], -1) for k in range(a.shape[0])]
    return dict(s5_re=flat(s5_re), s5_im=flat(s5_im), rg_h=flat(rg_h), rg_conv=flat(rg_conv),
                ffn_conv=flat(ffn_conv))


def _stack_states(new, like):
    return tuple(jnp.stack(new[k]).reshape((len(new[k]), new[k][0].shape[0]) + like[k].shape[2:])
                 for k in ("s5_re", "s5_im", "rg_h", "rg_conv", "ffn_conv"))


def kernel(x_prompt, x_sample, state_s5_re, state_s5_im, state_rg_h, state_rg_conv, state_ffn_conv,
           meta_tokens, s5_w_in, s5_lam_re, s5_lam_im, s5_log_step, s5_b_re, s5_b_im, s5_c_re,
           s5_c_im, s5_d, s5_w_out, rg_w_in, rg_conv_w, rg_conv_b, rg_w_gates, rg_b_gates, rg_lam,
           rg_w_out, ffn_w_up, ffn_conv_w, ffn_conv_b, ffn_w_down, ln_g, ln_b):
    depth = ffn_w_up.shape[0]
    alpha = (2 * depth) ** 0.25
    bp, seq, d_model = x_prompt.shape
    bs, dec_seq, _ = x_sample.shape
    n_meta = meta_tokens.shape[0]
    dt = x_prompt.dtype

    params = dict(
        s5=[_s5_params(s5_w_in[j], s5_lam_re[j], s5_lam_im[j], s5_log_step[j], s5_b_re[j], s5_b_im[j],
                       s5_c_re[j], s5_c_im[j], s5_d[j], s5_w_out[j]) for j in range(s5_w_in.shape[0])],
        rg=[dict(w_in=rg_w_in[j].astype(BF16), conv_w=rg_conv_w[j], conv_b=rg_conv_b[j].reshape(1, -1),
                 w_gates=rg_w_gates[j].astype(BF16), b_gates=rg_b_gates[j].reshape(1, -1),
                 lam=rg_lam[j].astype(F32).reshape(1, -1), w_out=rg_w_out[j].astype(BF16))
            for j in range(rg_w_in.shape[0])],
        ffn=[dict(w_up=ffn_w_up[i].astype(BF16), conv_w=ffn_conv_w[i], conv_b=ffn_conv_b[i].reshape(1, -1),
                  w_down=ffn_w_down[i].astype(BF16)) for i in range(depth)],
        ln_g=ln_g, ln_b=ln_b)
    like = dict(s5_re=state_s5_re, s5_im=state_s5_im, rg_h=state_rg_h, rg_conv=state_rg_conv,
                ffn_conv=state_ffn_conv)

    h_m = jnp.broadcast_to(meta_tokens[:, None].astype(dt), (n_meta, bp, d_model)).reshape(n_meta * bp, d_model)
    _, meta_states = _trunk(h_m, None, params, batch=bp, seq_len=n_meta, alpha=alpha)

    h_p = x_prompt.transpose(1, 0, 2).reshape(seq * bp, d_model)
    yp, new_p = _trunk(h_p, meta_states, params, batch=bp, seq_len=seq, alpha=alpha)
    y_prompt = yp.reshape(seq, bp, d_model).transpose(1, 0, 2)

    h_s = x_sample.transpose(1, 0, 2).reshape(dec_seq * bs, d_model)
    ys, new_s = _trunk(h_s, _flat_states(state_s5_re, state_s5_im, state_rg_h, state_rg_conv,
                                         state_ffn_conv),
                       params, batch=bs, seq_len=dec_seq, alpha=alpha)
    y_sample = ys.reshape(dec_seq, bs, d_model).transpose(1, 0, 2)

    return (y_prompt, y_sample) + _stack_states(new_p, like) + _stack_states(new_s, like)
```

```python
import functools

import jax
import jax.numpy as jnp
from jax import lax
from jax.experimental import pallas as pl
from jax.experimental.pallas import tpu as pltpu

RG_BLOCKS = 4
RG_C = 8.0
LN_EPS = 1e-5

SUBLANES = 8
S5_CHUNK_GROUPS = 8
FF_TILE = 256
BLOCK_ROWS = 1024
SUB_ROWS = 512
VMEM_LIMIT_BYTES = 56 * 1024 * 1024

BF16 = jnp.bfloat16
F32 = jnp.float32


def _dot(a, b):
    return jnp.dot(a, b, preferred_element_type=F32)


def _layer_norm(z, g, b):
    mu = jnp.mean(z, axis=-1, keepdims=True)
    zc = z - mu
    var = jnp.mean(zc * zc, axis=-1, keepdims=True)
    return zc * lax.rsqrt(var + LN_EPS) * g + b


def _largest_divisor(n, limit):
    return max(d for d in range(1, n + 1) if n % d == 0 and d <= limit)


def _tiling(seq_len, batch, min_steps):
    steps = _largest_divisor(seq_len, max(BLOCK_ROWS // batch, min_steps))
    sub = _largest_divisor(steps, max(SUB_ROWS // batch, min_steps))
    assert sub >= min_steps, (seq_len, batch, min_steps)
    return steps, sub


def _whole(a):
    return a, pl.BlockSpec(a.shape, lambda i: (0,) * a.ndim, pipeline_mode=pl.Buffered(1))


def _pick(a, fixed):
    block = tuple(None if d in fixed else n for d, n in enumerate(a.shape))
    index = tuple(fixed.get(d, 0) for d in range(a.ndim))
    return a, pl.BlockSpec(block, lambda i: index, pipeline_mode=pl.Buffered(1))


def _row_specs(batch, steps, d_model, batch_major):
    if batch_major:
        return pl.BlockSpec((batch, steps, d_model), lambda i: (0, i, 0))
    return pl.BlockSpec((steps * batch, d_model), lambda i: (i, 0))


def _load_rows(x_ref, t0, n_steps, batch, batch_major):
    if batch_major:
        return jnp.concatenate([x_ref[:, t0 + t, :] for t in range(n_steps)], axis=0)
    return x_ref[t0 * batch:(t0 + n_steps) * batch, :]


def _store_rows(out_ref, t0, n_steps, batch, batch_major, val):
    if batch_major:
        for t in range(n_steps):
            out_ref[:, t0 + t, :] = val[t * batch:(t + 1) * batch, :]
    else:
        out_ref[t0 * batch:(t0 + n_steps) * batch, :] = val


def _init_carry(carry_refs, init_refs):
    @pl.when(pl.program_id(0) == 0)
    def _():
        for k, ref in enumerate(carry_refs):
            ref[...] = init_refs[k][...] if init_refs else jnp.zeros_like(ref)


def _slab_scan(step_fn, carry, n_steps, batch):
    n_groups = batch // SUBLANES
    outs = [[None] * n_groups for _ in range(n_steps)]
    carry = list(carry)
    for t in range(n_steps):
        for rg in range(n_groups):
            row = t * batch + rg * SUBLANES
            carry[rg], outs[t][rg] = step_fn(carry[rg], row)
    n_out = len(outs[0][0])
    stacked = [jnp.concatenate([outs[t][rg][k] for t in range(n_steps) for rg in range(n_groups)], axis=0)
               for k in range(n_out)]
    return carry, stacked


def _causal_conv(up, hist_refs, col, cw, cb, batch):
    rows, width = up.shape
    n_hist = len(hist_refs)
    ext = jnp.concatenate([r[:, col:col + width] for r in hist_refs] + [up], axis=0)
    for k in range(n_hist):
        hist_refs[k][:, col:col + width] = ext[rows + k * batch:rows + (k + 1) * batch]
    y = cb + cw[0] * ext[0:rows]
    for k in range(1, n_hist + 1):
        y = y + cw[k] * ext[k * batch:k * batch + rows]
    return y


def _call(kern, name, x, x_spec, operands, out_shapes, out_specs, scratch=()):
    arrays, specs = zip(*operands)
    return pl.pallas_call(
        kern,
        grid=(x_spec_grid(x, x_spec),),
        in_specs=[x_spec, *specs],
        out_specs=out_specs,
        out_shape=out_shapes,
        scratch_shapes=list(scratch),
        compiler_params=pltpu.CompilerParams(
            dimension_semantics=("arbitrary",), vmem_limit_bytes=VMEM_LIMIT_BYTES),
        name=name,
    )(x, *arrays)


def x_spec_grid(x, x_spec):
    dim = 1 if x.ndim == 3 else 0
    return x.shape[dim] // x_spec.block_shape[dim]


def _carry_spec(shape):
    return pl.BlockSpec(shape, lambda i: (0,) * len(shape))


def _s5_kernel(*refs, layer, ln_row, batch, steps, sub_steps, alpha, n_init, x_batch_major):
    x_ref, refs = refs[0], refs[1:]
    init_refs, refs = refs[:n_init], refs[n_init:]
    (win_ref, bblk_ref, cblk_ref, are_ref, aim_ref, d_ref, wout_ref, g_ref, b_ref,
     out_ref, sre_ref, sim_ref) = refs
    d_model = out_ref.shape[1]
    n_chunks, cin, cst2 = bblk_ref.shape
    cst = cst2 // 2
    sub_rows = sub_steps * batch
    n_groups = batch // SUBLANES
    _init_carry((sre_ref, sim_ref), init_refs)
    ln_g, ln_b = g_ref[ln_row[0], ln_row[1]:ln_row[1] + 1, :], b_ref[ln_row[0], ln_row[1]:ln_row[1] + 1, :]
    d_skip = d_ref[layer:layer + 1, :]

    def rows_of(ref, rg, c):
        return ref[rg * SUBLANES:(rg + 1) * SUBLANES, c * cst:(c + 1) * cst]

    state = [[(rows_of(sre_ref, rg, c), rows_of(sim_ref, rg, c)) for rg in range(n_groups)]
             for c in range(n_chunks)]
    for s in range(steps // sub_steps):
        x = _load_rows(x_ref, s * sub_steps, sub_steps, batch, x_batch_major)
        u = _dot(x.astype(BF16), win_ref[...])
        ys = []
        for c in range(n_chunks):
            bu = _dot(u[:, c * cin:(c + 1) * cin].astype(BF16), bblk_ref[c])
            a_re = jnp.broadcast_to(are_ref[c:c + 1, :], (SUBLANES, cst))
            a_im = jnp.broadcast_to(aim_ref[c:c + 1, :], (SUBLANES, cst))

            def step(carry, row, bu=bu, a_re=a_re, a_im=a_im):
                s_re, s_im = carry
                n_re = a_re * s_re - a_im * s_im + bu[row:row + SUBLANES, 0:cst]
                n_im = a_re * s_im + a_im * s_re + bu[row:row + SUBLANES, cst:cst2]
                return (n_re, n_im), (n_re, n_im)

            state[c], (st_re, st_im) = _slab_scan(step, state[c], sub_steps, batch)
            st = jnp.concatenate([st_re, st_im], axis=1).astype(BF16)
            ys.append(_dot(st, cblk_ref[c]))
        y = jnp.concatenate(ys, axis=1) + d_skip * u
        vg = _dot(jax.nn.gelu(y).astype(BF16), wout_ref[...])
        mix = vg[:, :d_model] * jax.nn.sigmoid(vg[:, d_model:])
        out_ref[s * sub_rows:(s + 1) * sub_rows, :] = _layer_norm(alpha * x + mix, ln_g, ln_b)
    for c in range(n_chunks):
        for rg in range(n_groups):
            sre_ref[rg * SUBLANES:(rg + 1) * SUBLANES, c * cst:(c + 1) * cst] = state[c][rg][0]
            sim_ref[rg * SUBLANES:(rg + 1) * SUBLANES, c * cst:(c + 1) * cst] = state[c][rg][1]


def _s5_layer(h, init, p, j, ln_row, *, batch, seq_len, alpha):
    batch_major = h.ndim == 3
    d_model = h.shape[-1]
    steps, sub_steps = _tiling(seq_len, batch, 1)
    n_state = p["a_re"].shape[1] * p["a_re"].shape[2]
    operands = list(init) + [
        _pick(p["s5_w_in"], {0: j}), _pick(p["bblk"], {0: j}), _pick(p["cblk"], {0: j}),
        _pick(p["a_re"], {0: j}), _pick(p["a_im"], {0: j}), _whole(p["s5_d"]),
        _pick(p["s5_w_out"], {0: j}), _whole(p["ln_g"]), _whole(p["ln_b"])]
    kern = functools.partial(_s5_kernel, layer=j, ln_row=ln_row, batch=batch, steps=steps,
                             sub_steps=sub_steps, alpha=alpha, n_init=len(init), x_batch_major=batch_major)
    return _call(
        kern, "s5_mixer_ln", h, _row_specs(batch, steps, d_model, batch_major), operands,
        [jax.ShapeDtypeStruct((seq_len * batch, d_model), F32),
         jax.ShapeDtypeStruct((batch, n_state), F32), jax.ShapeDtypeStruct((batch, n_state), F32)],
        [_row_specs(batch, steps, d_model, False), _carry_spec((batch, n_state)), _carry_spec((batch, n_state))])


def _s5_params(lam_re, lam_im, log_step, b_re, b_im, c_re, c_im):
    n_layers, n_groups, n_state, gsz = b_re.shape
    gc = S5_CHUNK_GROUPS
    nc = n_groups // gc
    step = jnp.exp(log_step.astype(F32))[..., None]
    lr, li = lam_re.astype(F32), lam_im.astype(F32)
    mag = jnp.exp(lr * step)
    ab_re, ab_im = mag * jnp.cos(li * step), mag * jnp.sin(li * step)
    den = lr * lr + li * li
    nr, ni = ab_re - 1.0, ab_im
    q_re = ((nr * lr + ni * li) / den)[..., None]
    q_im = ((ni * lr - nr * li) / den)[..., None]
    br, bi = b_re.astype(F32), b_im.astype(F32)
    bb_re = q_re * br - q_im * bi
    bb_im = q_re * bi + q_im * br
    eye = jnp.eye(gc, dtype=F32)[None, None, :, :, None, None]

    def b_block(bb):
        t = bb.reshape(n_layers, nc, gc, 1, n_state, gsz) * eye
        return t.transpose(0, 1, 2, 5, 3, 4).reshape(n_layers, nc, gc * gsz, gc * n_state)

    def c_block(cc):
        t = cc.astype(F32).reshape(n_layers, nc, gc, 1, gsz, n_state) * eye
        return t.transpose(0, 1, 2, 5, 3, 4).reshape(n_layers, nc, gc * n_state, gc * gsz)

    bblk = jnp.concatenate([b_block(bb_re), b_block(bb_im)], axis=3).astype(BF16)
    cblk = jnp.concatenate([c_block(c_re), -c_block(c_im)], axis=2).astype(BF16)
    return dict(bblk=bblk, cblk=cblk, a_re=ab_re.reshape(n_layers, nc, gc * n_state),
                a_im=ab_im.reshape(n_layers, nc, gc * n_state))


def _softplus(z):
    return jnp.maximum(z, 0.0) + jnp.log1p(jnp.exp(-jnp.abs(z)))


def _rg_kernel(*refs, layer, ln_row, batch, steps, sub_steps, alpha, n_init):
    x_ref, refs = refs[0], refs[1:]
    init_refs, refs = refs[:n_init], refs[n_init:]
    (win_ref, cw_ref, cb_ref, wg_ref, bg_ref, lam_ref, wout_ref, g_ref, b_ref, out_ref, hn_ref) = refs[:11]
    hist_refs = refs[11:]
    width = hn_ref.shape[1]
    blk = width // RG_BLOCKS
    sub_rows = sub_steps * batch
    n_groups = batch // SUBLANES
    _init_carry((hn_ref, *hist_refs), init_refs)
    ln_g, ln_b = g_ref[ln_row[0], ln_row[1]:ln_row[1] + 1, :], b_ref[ln_row[0], ln_row[1]:ln_row[1] + 1, :]
    cw = [cw_ref[layer, k:k + 1, :] for k in range(cw_ref.shape[1])]
    cb = cb_ref[layer:layer + 1, :]
    b_r, b_i = bg_ref[layer:layer + 1, :width], bg_ref[layer:layer + 1, width:]

    neg_c_softplus = -RG_C * _softplus(-lam_ref[layer:layer + 1, :])
    h_state = [hn_ref[rg * SUBLANES:(rg + 1) * SUBLANES, :] for rg in range(n_groups)]
    for s in range(steps // sub_steps):
        x = x_ref[s * sub_rows:(s + 1) * sub_rows, :]
        z = _dot(x.astype(BF16), win_ref[...])
        gate = jax.nn.gelu(z[:, :width])
        xc = _causal_conv(z[:, width:], hist_refs, 0, cw, cb, batch)
        rs, is_ = [], []
        for n in range(RG_BLOCKS):
            gts = _dot(xc[:, n * blk:(n + 1) * blk].astype(BF16), wg_ref[n])
            rs.append(gts[:, :blk])
            is_.append(gts[:, blk:])
        r = jax.nn.sigmoid(jnp.concatenate(rs, axis=1) + b_r)
        ig = jax.nn.sigmoid(jnp.concatenate(is_, axis=1) + b_i)
        log_a = r * neg_c_softplus
        a = jnp.exp(log_a)
        th = jnp.tanh(log_a)
        bb = jnp.sqrt(-2.0 * th / (1.0 - th)) * (ig * xc)

        def step(h, row, a=a, bb=bb):
            h = a[row:row + SUBLANES, :] * h + bb[row:row + SUBLANES, :]
            return h, (h,)

        h_state, (hs,) = _slab_scan(step, h_state, sub_steps, batch)
        y = _dot((hs * gate).astype(BF16), wout_ref[...])
        out_ref[s * sub_rows:(s + 1) * sub_rows, :] = _layer_norm(alpha * x + y, ln_g, ln_b)
    for rg in range(n_groups):
        hn_ref[rg * SUBLANES:(rg + 1) * SUBLANES, :] = h_state[rg]


def _rg_layer(h, init, p, j, ln_row, *, batch, seq_len, alpha):
    rows, d_model = h.shape
    width = p["rg_w_out"].shape[1]
    n_taps = p["rg_conv_w"].shape[1]
    steps, sub_steps = _tiling(seq_len, batch, n_taps - 1)
    operands = list(init) + [
        _pick(p["rg_w_in"], {0: j}), _whole(p["rg_conv_w"]), _whole(p["rg_conv_b"]),
        _pick(p["rg_w_gates"], {0: j}), _whole(p["rg_b_gates"]), _whole(p["rg_lam"]),
        _pick(p["rg_w_out"], {0: j}), _whole(p["ln_g"]), _whole(p["ln_b"])]
    kern = functools.partial(_rg_kernel, layer=j, ln_row=ln_row, batch=batch, steps=steps,
                             sub_steps=sub_steps, alpha=alpha, n_init=len(init))
    n_carry = n_taps
    return _call(
        kern, "rglru_mixer_ln", h, _row_specs(batch, steps, d_model, False), operands,
        [jax.ShapeDtypeStruct((rows, d_model), F32)]
        + [jax.ShapeDtypeStruct((batch, width), F32)] * n_carry,
        [_row_specs(batch, steps, d_model, False)] + [_carry_spec((batch, width))] * n_carry)


def _ffn_kernel(*refs, layer, ln_row, batch, steps, sub_steps, alpha, n_init, out_batch_major):
    x_ref, refs = refs[0], refs[1:]
    init_refs, refs = refs[:n_init], refs[n_init:]
    wup_ref, cw_ref, cb_ref, wd_ref, g_ref, b_ref, out_ref = refs[:7]
    hist_refs, act_scr = refs[7:-1], refs[-1]
    ff = wd_ref.shape[0]
    n_taps = cw_ref.shape[1]
    sub_rows = sub_steps * batch
    _init_carry(hist_refs, init_refs)
    ln_g, ln_b = g_ref[ln_row[0], ln_row[1]:ln_row[1] + 1, :], b_ref[ln_row[0], ln_row[1]:ln_row[1] + 1, :]

    for s in range(steps // sub_steps):
        x = x_ref[s * sub_rows:(s + 1) * sub_rows, :]
        xb = x.astype(BF16)
        for j in range(ff // FF_TILE):
            val, gate = [
                _causal_conv(_dot(xb, wup_ref[:, col:col + FF_TILE]), hist_refs, col,
                             [cw_ref[layer, k:k + 1, col:col + FF_TILE] for k in range(n_taps)],
                             cb_ref[layer:layer + 1, col:col + FF_TILE], batch)
                for col in (j * FF_TILE, ff + j * FF_TILE)]
            act_scr[s % 2, :, j * FF_TILE:(j + 1) * FF_TILE] = (jax.nn.gelu(gate) * val).astype(BF16)
        y = _dot(act_scr[s % 2], wd_ref[...])
        _store_rows(out_ref, s * sub_steps, sub_steps, batch, out_batch_major,
                    _layer_norm(alpha * x + y, ln_g, ln_b))


def _ffn_layer(h, init, p, i, ln_row, *, batch, seq_len, alpha, out_batch_major):
    rows, d_model = h.shape
    ff = p["ffn_w_down"].shape[1]
    n_taps = p["ffn_conv_w"].shape[1]
    steps, sub_steps = _tiling(seq_len, batch, n_taps - 1)
    operands = list(init) + [
        _pick(p["ffn_w_up"], {0: i}), _whole(p["ffn_conv_w"]), _whole(p["ffn_conv_b"]),
        _pick(p["ffn_w_down"], {0: i}), _whole(p["ln_g"]), _whole(p["ln_b"])]
    kern = functools.partial(_ffn_kernel, layer=i, ln_row=ln_row, batch=batch, steps=steps,
                             sub_steps=sub_steps, alpha=alpha, n_init=len(init),
                             out_batch_major=out_batch_major)
    out_shape = (batch, seq_len, d_model) if out_batch_major else (rows, d_model)
    return _call(
        kern, "convffn_ln", h, _row_specs(batch, steps, d_model, False), operands,
        [jax.ShapeDtypeStruct(out_shape, F32)] + [jax.ShapeDtypeStruct((batch, 2 * ff), F32)] * (n_taps - 1),
        [_row_specs(batch, steps, d_model, out_batch_major)] + [_carry_spec((batch, 2 * ff))] * (n_taps - 1),
        scratch=[pltpu.VMEM((2, sub_steps * batch, ff), BF16)])


def _trunk(x, init, p, *, alpha):
    batch, seq_len, _ = x.shape
    depth = p["ffn_w_up"].shape[0]
    new = dict(s5_re=[], s5_im=[], rg_h=[], rg_conv=[], ffn_conv=[])
    kw = dict(batch=batch, seq_len=seq_len, alpha=alpha)
    h = x
    for i in range(depth):
        j = i // 2
        if i % 2 == 0:
            st = () if init is None else (init["s5_re"][j], init["s5_im"][j])
            h, sr, si = _s5_layer(h, st, p, j, (i, 0), **kw)
            new["s5_re"].append(sr)
            new["s5_im"].append(si)
        else:
            st = () if init is None else (init["rg_h"][j], *init["rg_conv"][j])
            h, hn, *hist = _rg_layer(h, st, p, j, (i, 0), **kw)
            new["rg_h"].append(hn)
            new["rg_conv"].append(hist)
        st = () if init is None else tuple(init["ffn_conv"][i])
        h, *hist = _ffn_layer(h, st, p, i, (i, 1), out_batch_major=(i == depth - 1), **kw)
        new["ffn_conv"].append(hist)
    return h, new


def _carried(new):
    one = lambda a: _whole(a)
    return dict(s5_re=[one(a) for a in new["s5_re"]], s5_im=[one(a) for a in new["s5_im"]],
                rg_h=[one(a) for a in new["rg_h"]],
                rg_conv=[[one(a) for a in hist] for hist in new["rg_conv"]],
                ffn_conv=[[one(a) for a in hist] for hist in new["ffn_conv"]])


def _given(s5_re, s5_im, rg_h, rg_conv, ffn_conv):
    flat = lambda a: [_whole(a[k].reshape(a.shape[1], -1)) for k in range(a.shape[0])]

    def steps(a):
        t = a.transpose(0, 2, 1, 3)
        return [[_pick(t, {0: n, 1: k}) for k in range(t.shape[1])] for n in range(t.shape[0])]

    return dict(s5_re=flat(s5_re), s5_im=flat(s5_im), rg_h=[_pick(rg_h, {0: n}) for n in range(rg_h.shape[0])],
                rg_conv=steps(rg_conv), ffn_conv=steps(ffn_conv))


def _stack_states(new, s5_shape):
    s5 = lambda k: jnp.stack(new[k]).reshape((len(new[k]), new[k][0].shape[0]) + s5_shape)
    conv = lambda k: jnp.stack([jnp.stack(hist, axis=1) for hist in new[k]])
    return (s5("s5_re"), s5("s5_im"), jnp.stack(new["rg_h"]), conv("rg_conv"), conv("ffn_conv"))


def kernel(x_prompt, x_sample, state_s5_re, state_s5_im, state_rg_h, state_rg_conv, state_ffn_conv,
           meta_tokens, s5_w_in, s5_lam_re, s5_lam_im, s5_log_step, s5_b_re, s5_b_im, s5_c_re,
           s5_c_im, s5_d, s5_w_out, rg_w_in, rg_conv_w, rg_conv_b, rg_w_gates, rg_b_gates, rg_lam,
           rg_w_out, ffn_w_up, ffn_conv_w, ffn_conv_b, ffn_w_down, ln_g, ln_b):
    depth = ffn_w_up.shape[0]
    alpha = (2 * depth) ** 0.25
    bp = x_prompt.shape[0]
    dt = x_prompt.dtype

    p = dict(
        s5_w_in=s5_w_in.astype(BF16), s5_w_out=s5_w_out.astype(BF16), s5_d=s5_d,
        rg_w_in=rg_w_in.astype(BF16), rg_conv_w=rg_conv_w, rg_conv_b=rg_conv_b,
        rg_w_gates=rg_w_gates.astype(BF16), rg_b_gates=rg_b_gates, rg_lam=rg_lam,
        rg_w_out=rg_w_out.astype(BF16),
        ffn_w_up=ffn_w_up.astype(BF16), ffn_conv_w=ffn_conv_w, ffn_conv_b=ffn_conv_b,
        ffn_w_down=ffn_w_down.astype(BF16), ln_g=ln_g, ln_b=ln_b,
        **_s5_params(s5_lam_re, s5_lam_im, s5_log_step, s5_b_re, s5_b_im, s5_c_re, s5_c_im))
    s5_shape = state_s5_re.shape[2:]

    x_meta = jnp.broadcast_to(meta_tokens[None].astype(dt), (bp,) + meta_tokens.shape)
    _, meta_states = _trunk(x_meta, None, p, alpha=alpha)
    y_prompt, new_p = _trunk(x_prompt, _carried(meta_states), p, alpha=alpha)
    y_sample, new_s = _trunk(x_sample, _given(state_s5_re, state_s5_im, state_rg_h, state_rg_conv,
                                              state_ffn_conv), p, alpha=alpha)
    return (y_prompt, y_sample) + _stack_states(new_p, s5_shape) + _stack_states(new_s, s5_shape)
```

```python
import functools
import math

import jax
import jax.numpy as jnp
from jax import lax
from jax.experimental import pallas as pl
from jax.experimental.pallas import tpu as pltpu

RG_BLOCKS = 4
RG_C = 8.0
LN_EPS = 1e-5

SUBLANES = 8
S5_CHUNK_GROUPS = 8
MXU_COLS = 256
FF_TILE = MXU_COLS
BLOCK_ROWS = 1024
SUB_ROWS = 512
VMEM_LIMIT_BYTES = 56 * 1024 * 1024

BF16 = jnp.bfloat16
F32 = jnp.float32


def _dot(a, b):
    return jnp.dot(a, b, preferred_element_type=F32)


def _layer_norm(z, g, b):
    mu = jnp.mean(z, axis=-1, keepdims=True)
    zc = z - mu
    var = jnp.mean(zc * zc, axis=-1, keepdims=True)
    return zc * lax.rsqrt(var + LN_EPS) * g + b


def _largest_divisor(n, limit):
    return max(d for d in range(1, n + 1) if n % d == 0 and d <= limit)


def _tiling(seq_len, batch, min_steps):
    steps = _largest_divisor(seq_len, max(BLOCK_ROWS // batch, min_steps))
    sub = _largest_divisor(steps, max(SUB_ROWS // batch, min_steps))
    assert sub >= min_steps, (seq_len, batch, min_steps)
    return steps, sub


def _whole(a):
    return a, pl.BlockSpec(a.shape, lambda i: (0,) * a.ndim, pipeline_mode=pl.Buffered(1))


def _pick(a, fixed):
    block = tuple(None if d in fixed else n for d, n in enumerate(a.shape))
    index = tuple(fixed.get(d, 0) for d in range(a.ndim))
    return a, pl.BlockSpec(block, lambda i: index, pipeline_mode=pl.Buffered(1))


def _row_specs(batch, steps, d_model, batch_major):
    if batch_major:
        return pl.BlockSpec((batch, steps, d_model), lambda i: (0, i, 0))
    return pl.BlockSpec((steps * batch, d_model), lambda i: (i, 0))


def _load_rows(x_ref, t0, n_steps, batch, batch_major):
    if batch_major:
        return jnp.concatenate([x_ref[:, t0 + t, :] for t in range(n_steps)], axis=0)
    return x_ref[t0 * batch:(t0 + n_steps) * batch, :]


def _store_rows(out_ref, t0, n_steps, batch, batch_major, val):
    if batch_major:
        for t in range(n_steps):
            out_ref[:, t0 + t, :] = val[t * batch:(t + 1) * batch, :]
    else:
        out_ref[t0 * batch:(t0 + n_steps) * batch, :] = val


def _init_carry(carry_refs, init_refs):
    @pl.when(pl.program_id(0) == 0)
    def _():
        for k, ref in enumerate(carry_refs):
            ref[...] = init_refs[k][...] if init_refs else jnp.zeros_like(ref)


def _slab_scan(step_fn, carry, n_steps, batch):
    n_groups = batch // SUBLANES
    outs = [[None] * n_groups for _ in range(n_steps)]
    carry = list(carry)
    for t in range(n_steps):
        for rg in range(n_groups):
            row = t * batch + rg * SUBLANES
            carry[rg], outs[t][rg] = step_fn(carry[rg], row)
    n_out = len(outs[0][0])
    stacked = [jnp.concatenate([outs[t][rg][k] for t in range(n_steps) for rg in range(n_groups)], axis=0)
               for k in range(n_out)]
    return carry, stacked


def _causal_conv(up, hist_refs, col, cw, cb, batch):
    rows, width = up.shape
    n_hist = len(hist_refs)
    ext = jnp.concatenate([r[:, col:col + width] for r in hist_refs] + [up], axis=0)
    for k in range(n_hist):
        hist_refs[k][:, col:col + width] = ext[rows + k * batch:rows + (k + 1) * batch]
    y = cb + cw[0] * ext[0:rows]
    for k in range(1, n_hist + 1):
        y = y + cw[k] * ext[k * batch:k * batch + rows]
    return y


def _call(kern, name, x, x_spec, operands, out_shapes, out_specs, scratch=()):
    arrays, specs = zip(*operands)
    return pl.pallas_call(
        kern,
        grid=(x_spec_grid(x, x_spec),),
        in_specs=[x_spec, *specs],
        out_specs=out_specs,
        out_shape=out_shapes,
        scratch_shapes=list(scratch),
        compiler_params=pltpu.CompilerParams(
            dimension_semantics=("arbitrary",), vmem_limit_bytes=VMEM_LIMIT_BYTES),
        name=name,
    )(x, *arrays)


def x_spec_grid(x, x_spec):
    dim = 1 if x.ndim == 3 else 0
    return x.shape[dim] // x_spec.block_shape[dim]


def _carry_spec(shape):
    return pl.BlockSpec(shape, lambda i: (0,) * len(shape))


def _s5_kernel(*refs, layer, ln_row, batch, steps, sub_steps, alpha, n_init, x_batch_major):
    x_ref, refs = refs[0], refs[1:]
    init_refs, refs = refs[:n_init], refs[n_init:]
    (win_ref, bblk_ref, cblk_ref, are_ref, aim_ref, d_ref, wout_ref, g_ref, b_ref,
     out_ref, sre_ref, sim_ref) = refs
    d_model = out_ref.shape[1]
    n_chunks, cin, cst2 = bblk_ref.shape
    cst = cst2 // 2
    sub_rows = sub_steps * batch
    n_groups = batch // SUBLANES
    _init_carry((sre_ref, sim_ref), init_refs)
    ln_g, ln_b = g_ref[ln_row[0], ln_row[1]:ln_row[1] + 1, :], b_ref[ln_row[0], ln_row[1]:ln_row[1] + 1, :]
    d_skip = d_ref[layer:layer + 1, :]

    def rows_of(ref, rg, c):
        return ref[rg * SUBLANES:(rg + 1) * SUBLANES, c * cst:(c + 1) * cst]

    state = [[(rows_of(sre_ref, rg, c), rows_of(sim_ref, rg, c)) for rg in range(n_groups)]
             for c in range(n_chunks)]
    def drive(u, c):
        return _dot(u[:, c * cin:(c + 1) * cin].astype(BF16), bblk_ref[c])

    def scan_readout(bu, c):
        a_re = jnp.broadcast_to(are_ref[c:c + 1, :], (SUBLANES, cst))
        a_im = jnp.broadcast_to(aim_ref[c:c + 1, :], (SUBLANES, cst))

        def step(carry, row):
            s_re, s_im = carry
            n_re = a_re * s_re - a_im * s_im + bu[row:row + SUBLANES, 0:cst]
            n_im = a_re * s_im + a_im * s_re + bu[row:row + SUBLANES, cst:cst2]
            return (n_re, n_im), (n_re, n_im)

        state[c], (st_re, st_im) = _slab_scan(step, state[c], sub_steps, batch)
        st = jnp.concatenate([st_re, st_im], axis=1).astype(BF16)
        return _dot(st, cblk_ref[c])

    def finish(s, x, u, ys):
        y = jnp.concatenate(ys, axis=1) + d_skip * u
        vg = _dot(jax.nn.gelu(y).astype(BF16), wout_ref[...])
        mix = vg[:, :d_model] * jax.nn.sigmoid(vg[:, d_model:])
        out_ref[s * sub_rows:(s + 1) * sub_rows, :] = _layer_norm(alpha * x + mix, ln_g, ln_b)

    pending = None
    for s in range(steps // sub_steps):
        x = _load_rows(x_ref, s * sub_steps, sub_steps, batch, x_batch_major)
        u = _dot(x.astype(BF16), win_ref[...])
        ys = []
        bu_next = drive(u, 0)
        for c in range(n_chunks):
            bu = bu_next
            if c + 1 < n_chunks:
                bu_next = drive(u, c + 1)
            ys.append(scan_readout(bu, c))
            if c == 0 and pending is not None:
                finish(*pending)
        pending = (s, x, u, ys)
    finish(*pending)
    for c in range(n_chunks):
        for rg in range(n_groups):
            sre_ref[rg * SUBLANES:(rg + 1) * SUBLANES, c * cst:(c + 1) * cst] = state[c][rg][0]
            sim_ref[rg * SUBLANES:(rg + 1) * SUBLANES, c * cst:(c + 1) * cst] = state[c][rg][1]


def _s5_layer(h, init, p, j, ln_row, *, batch, seq_len, alpha):
    batch_major = h.ndim == 3
    d_model = h.shape[-1]
    steps, sub_steps = _tiling(seq_len, batch, 1)
    n_state = p["a_re"].shape[1] * p["a_re"].shape[2]
    operands = list(init) + [
        _pick(p["s5_w_in"], {0: j}), _pick(p["bblk"], {0: j}), _pick(p["cblk"], {0: j}),
        _pick(p["a_re"], {0: j}), _pick(p["a_im"], {0: j}), _whole(p["s5_d"]),
        _pick(p["s5_w_out"], {0: j}), _whole(p["ln_g"]), _whole(p["ln_b"])]
    kern = functools.partial(_s5_kernel, layer=j, ln_row=ln_row, batch=batch, steps=steps,
                             sub_steps=sub_steps, alpha=alpha, n_init=len(init), x_batch_major=batch_major)
    return _call(
        kern, "s5_mixer_ln", h, _row_specs(batch, steps, d_model, batch_major), operands,
        [jax.ShapeDtypeStruct((seq_len * batch, d_model), F32),
         jax.ShapeDtypeStruct((batch, n_state), F32), jax.ShapeDtypeStruct((batch, n_state), F32)],
        [_row_specs(batch, steps, d_model, False), _carry_spec((batch, n_state)), _carry_spec((batch, n_state))])


def _s5_params(lam_re, lam_im, log_step, b_re, b_im, c_re, c_im):
    n_layers, n_groups, n_state, gsz = b_re.shape
    gc = S5_CHUNK_GROUPS
    nc = n_groups // gc
    step = jnp.exp(log_step.astype(F32))[..., None]
    lr, li = lam_re.astype(F32), lam_im.astype(F32)
    mag = jnp.exp(lr * step)
    ab_re, ab_im = mag * jnp.cos(li * step), mag * jnp.sin(li * step)
    den = lr * lr + li * li
    nr, ni = ab_re - 1.0, ab_im
    q_re = ((nr * lr + ni * li) / den)[..., None]
    q_im = ((ni * lr - nr * li) / den)[..., None]
    br, bi = b_re.astype(F32), b_im.astype(F32)
    bb_re = q_re * br - q_im * bi
    bb_im = q_re * bi + q_im * br
    eye = jnp.eye(gc, dtype=F32)[None, None, :, :, None, None]

    def b_block(bb):
        t = bb.reshape(n_layers, nc, gc, 1, n_state, gsz) * eye
        return t.transpose(0, 1, 2, 5, 3, 4).reshape(n_layers, nc, gc * gsz, gc * n_state)

    def c_block(cc):
        t = cc.astype(F32).reshape(n_layers, nc, gc, 1, gsz, n_state) * eye
        return t.transpose(0, 1, 2, 5, 3, 4).reshape(n_layers, nc, gc * n_state, gc * gsz)

    bblk = jnp.concatenate([b_block(bb_re), b_block(bb_im)], axis=3).astype(BF16)
    cblk = jnp.concatenate([c_block(c_re), -c_block(c_im)], axis=2).astype(BF16)
    return dict(bblk=bblk, cblk=cblk, a_re=ab_re.reshape(n_layers, nc, gc * n_state),
                a_im=ab_im.reshape(n_layers, nc, gc * n_state))


def _softplus(z):
    return jnp.maximum(z, 0.0) + jnp.log1p(jnp.exp(-jnp.abs(z)))


def _rg_kernel(*refs, layer, ln_row, batch, steps, sub_steps, alpha, n_init):
    x_ref, refs = refs[0], refs[1:]
    init_refs, refs = refs[:n_init], refs[n_init:]
    (win_ref, cw_ref, cb_ref, wg_ref, bg_ref, lam_ref, wout_ref, g_ref, b_ref, out_ref, hn_ref) = refs[:11]
    hist_refs = refs[11:]
    width = hn_ref.shape[1]
    blk = width // RG_BLOCKS
    sub_rows = sub_steps * batch
    n_groups = batch // SUBLANES
    _init_carry((hn_ref, *hist_refs), init_refs)
    ln_g, ln_b = g_ref[ln_row[0], ln_row[1]:ln_row[1] + 1, :], b_ref[ln_row[0], ln_row[1]:ln_row[1] + 1, :]
    cw = [cw_ref[layer, k:k + 1, :] for k in range(cw_ref.shape[1])]
    cb = cb_ref[layer:layer + 1, :]
    b_r, b_i = bg_ref[layer:layer + 1, :width], bg_ref[layer:layer + 1, width:]

    c_softplus = RG_C * _softplus(-lam_ref[layer:layer + 1, :])
    h_state = [[hn_ref[rg * SUBLANES:(rg + 1) * SUBLANES, n * blk:(n + 1) * blk] for rg in range(n_groups)]
               for n in range(RG_BLOCKS)]

    def gate_block(xb, n):
        cols = slice(n * blk, (n + 1) * blk)
        zx = _dot(xb, win_ref[:, width + n * blk:width + (n + 1) * blk])
        xc = _causal_conv(zx, hist_refs, n * blk, [w[:, cols] for w in cw], cb[:, cols], batch)
        gts = _dot(xc.astype(BF16), wg_ref[n])
        r = jax.nn.sigmoid(gts[:, :blk] + b_r[:, cols])
        ig = jax.nn.sigmoid(gts[:, blk:] + b_i[:, cols])
        neg_log_a = r * c_softplus[:, cols]
        a = jnp.exp(-neg_log_a)
        th = jnp.tanh(neg_log_a)
        v = th / (1.0 + th)
        root = jnp.where(v > 0.0, v * lax.rsqrt(v), 0.0)
        bb = (math.sqrt(2.0) * root) * (ig * xc)

        def step(h, row):
            h = a[row:row + SUBLANES, :] * h + bb[row:row + SUBLANES, :]
            return h, (h,)

        h_state[n], (hs,) = _slab_scan(step, h_state[n], sub_steps, batch)
        gate = jax.nn.gelu(_dot(xb, win_ref[:, cols]))
        return (hs * gate).astype(BF16)

    def finish(s, x, hg):
        y = _dot(hg, wout_ref[...])
        out_ref[s * sub_rows:(s + 1) * sub_rows, :] = _layer_norm(alpha * x + y, ln_g, ln_b)

    pending = None
    for s in range(steps // sub_steps):
        x = x_ref[s * sub_rows:(s + 1) * sub_rows, :]
        xb = x.astype(BF16)
        blocks = []
        for n in range(RG_BLOCKS):
            blocks.append(gate_block(xb, n))
            if n == 0 and pending is not None:
                finish(*pending)
        pending = (s, x, jnp.concatenate(blocks, axis=1))
    finish(*pending)
    for n in range(RG_BLOCKS):
        for rg in range(n_groups):
            hn_ref[rg * SUBLANES:(rg + 1) * SUBLANES, n * blk:(n + 1) * blk] = h_state[n][rg]


def _rg_layer(h, init, p, j, ln_row, *, batch, seq_len, alpha):
    rows, d_model = h.shape
    width = p["rg_w_out"].shape[1]
    n_taps = p["rg_conv_w"].shape[1]
    steps, sub_steps = _tiling(seq_len, batch, n_taps - 1)
    operands = list(init) + [
        _pick(p["rg_w_in"], {0: j}), _whole(p["rg_conv_w"]), _whole(p["rg_conv_b"]),
        _pick(p["rg_w_gates"], {0: j}), _whole(p["rg_b_gates"]), _whole(p["rg_lam"]),
        _pick(p["rg_w_out"], {0: j}), _whole(p["ln_g"]), _whole(p["ln_b"])]
    kern = functools.partial(_rg_kernel, layer=j, ln_row=ln_row, batch=batch, steps=steps,
                             sub_steps=sub_steps, alpha=alpha, n_init=len(init))
    n_carry = n_taps
    return _call(
        kern, "rglru_mixer_ln", h, _row_specs(batch, steps, d_model, False), operands,
        [jax.ShapeDtypeStruct((rows, d_model), F32)]
        + [jax.ShapeDtypeStruct((batch, width), F32)] * n_carry,
        [_row_specs(batch, steps, d_model, False)] + [_carry_spec((batch, width))] * n_carry)


def _ffn_kernel(*refs, layer, ln_row, batch, steps, sub_steps, alpha, n_init, out_batch_major):
    x_ref, refs = refs[0], refs[1:]
    init_refs, refs = refs[:n_init], refs[n_init:]
    wup_ref, cw_ref, cb_ref, wd_ref, g_ref, b_ref, out_ref = refs[:7]
    hist_refs, act_scr = refs[7:-1], refs[-1]
    ff = wd_ref.shape[0]
    n_taps = cw_ref.shape[1]
    sub_rows = sub_steps * batch
    _init_carry(hist_refs, init_refs)
    ln_g, ln_b = g_ref[ln_row[0], ln_row[1]:ln_row[1] + 1, :], b_ref[ln_row[0], ln_row[1]:ln_row[1] + 1, :]

    def finish(s, x):
        y = _dot(act_scr[s % 2], wd_ref[...])
        _store_rows(out_ref, s * sub_steps, sub_steps, batch, out_batch_major,
                    _layer_norm(alpha * x + y, ln_g, ln_b))

    pending = None
    for s in range(steps // sub_steps):
        x = x_ref[s * sub_rows:(s + 1) * sub_rows, :]
        xb = x.astype(BF16)
        for j in range(ff // FF_TILE):
            val, gate = [
                _causal_conv(_dot(xb, wup_ref[:, col:col + FF_TILE]), hist_refs, col,
                             [cw_ref[layer, k:k + 1, col:col + FF_TILE] for k in range(n_taps)],
                             cb_ref[layer:layer + 1, col:col + FF_TILE], batch)
                for col in (j * FF_TILE, ff + j * FF_TILE)]
            act_scr[s % 2, :, j * FF_TILE:(j + 1) * FF_TILE] = (jax.nn.gelu(gate) * val).astype(BF16)
            if j == 0 and pending is not None:
                finish(*pending)
        pending = (s, x)
    finish(*pending)


def _ffn_layer(h, init, p, i, ln_row, *, batch, seq_len, alpha, out_batch_major):
    rows, d_model = h.shape
    ff = p["ffn_w_down"].shape[1]
    n_taps = p["ffn_conv_w"].shape[1]
    steps, sub_steps = _tiling(seq_len, batch, n_taps - 1)
    operands = list(init) + [
        _pick(p["ffn_w_up"], {0: i}), _whole(p["ffn_conv_w"]), _whole(p["ffn_conv_b"]),
        _pick(p["ffn_w_down"], {0: i}), _whole(p["ln_g"]), _whole(p["ln_b"])]
    kern = functools.partial(_ffn_kernel, layer=i, ln_row=ln_row, batch=batch, steps=steps,
                             sub_steps=sub_steps, alpha=alpha, n_init=len(init),
                             out_batch_major=out_batch_major)
    out_shape = (batch, seq_len, d_model) if out_batch_major else (rows, d_model)
    return _call(
        kern, "convffn_ln", h, _row_specs(batch, steps, d_model, False), operands,
        [jax.ShapeDtypeStruct(out_shape, F32)] + [jax.ShapeDtypeStruct((batch, 2 * ff), F32)] * (n_taps - 1),
        [_row_specs(batch, steps, d_model, out_batch_major)] + [_carry_spec((batch, 2 * ff))] * (n_taps - 1),
        scratch=[pltpu.VMEM((2, sub_steps * batch, ff), BF16)])


def _trunk(x, init, p, *, alpha):
    batch, seq_len, _ = x.shape
    depth = p["ffn_w_up"].shape[0]
    new = dict(s5_re=[], s5_im=[], rg_h=[], rg_conv=[], ffn_conv=[])
    kw = dict(batch=batch, seq_len=seq_len, alpha=alpha)
    h = x
    for i in range(depth):
        j = i // 2
        if i % 2 == 0:
            st = () if init is None else (init["s5_re"][j], init["s5_im"][j])
            h, sr, si = _s5_layer(h, st, p, j, (i, 0), **kw)
            new["s5_re"].append(sr)
            new["s5_im"].append(si)
        else:
            st = () if init is None else (init["rg_h"][j], *init["rg_conv"][j])
            h, hn, *hist = _rg_layer(h, st, p, j, (i, 0), **kw)
            new["rg_h"].append(hn)
            new["rg_conv"].append(hist)
        st = () if init is None else tuple(init["ffn_conv"][i])
        h, *hist = _ffn_layer(h, st, p, i, (i, 1), out_batch_major=(i == depth - 1), **kw)
        new["ffn_conv"].append(hist)
    return h, new


def _carried(new):
    one = lambda a: _whole(a)
    return dict(s5_re=[one(a) for a in new["s5_re"]], s5_im=[one(a) for a in new["s5_im"]],
                rg_h=[one(a) for a in new["rg_h"]],
                rg_conv=[[one(a) for a in hist] for hist in new["rg_conv"]],
                ffn_conv=[[one(a) for a in hist] for hist in new["ffn_conv"]])


def _given(s5_re, s5_im, rg_h, rg_conv, ffn_conv):
    flat = lambda a: [_whole(a[k].reshape(a.shape[1], -1)) for k in range(a.shape[0])]

    def steps(a):
        t = a.transpose(0, 2, 1, 3)
        return [[_pick(t, {0: n, 1: k}) for k in range(t.shape[1])] for n in range(t.shape[0])]

    return dict(s5_re=flat(s5_re), s5_im=flat(s5_im), rg_h=[_pick(rg_h, {0: n}) for n in range(rg_h.shape[0])],
                rg_conv=steps(rg_conv), ffn_conv=steps(ffn_conv))


def _stack_states(new, s5_shape):
    s5 = lambda k: jnp.stack(new[k]).reshape((len(new[k]), new[k][0].shape[0]) + s5_shape)
    conv = lambda k: jnp.stack([jnp.stack(hist, axis=1) for hist in new[k]])
    return (s5("s5_re"), s5("s5_im"), jnp.stack(new["rg_h"]), conv("rg_conv"), conv("ffn_conv"))


def kernel(x_prompt, x_sample, state_s5_re, state_s5_im, state_rg_h, state_rg_conv, state_ffn_conv,
           meta_tokens, s5_w_in, s5_lam_re, s5_lam_im, s5_log_step, s5_b_re, s5_b_im, s5_c_re,
           s5_c_im, s5_d, s5_w_out, rg_w_in, rg_conv_w, rg_conv_b, rg_w_gates, rg_b_gates, rg_lam,
           rg_w_out, ffn_w_up, ffn_conv_w, ffn_conv_b, ffn_w_down, ln_g, ln_b):
    depth = ffn_w_up.shape[0]
    alpha = (2 * depth) ** 0.25
    bp = x_prompt.shape[0]
    dt = x_prompt.dtype

    p = dict(
        s5_w_in=s5_w_in.astype(BF16), s5_w_out=s5_w_out.astype(BF16), s5_d=s5_d,
        rg_w_in=rg_w_in.astype(BF16), rg_conv_w=rg_conv_w, rg_conv_b=rg_conv_b,
        rg_w_gates=rg_w_gates.astype(BF16), rg_b_gates=rg_b_gates, rg_lam=rg_lam,
        rg_w_out=rg_w_out.astype(BF16),
        ffn_w_up=ffn_w_up.astype(BF16), ffn_conv_w=ffn_conv_w, ffn_conv_b=ffn_conv_b,
        ffn_w_down=ffn_w_down.astype(BF16), ln_g=ln_g, ln_b=ln_b,
        **_s5_params(s5_lam_re, s5_lam_im, s5_log_step, s5_b_re, s5_b_im, s5_c_re, s5_c_im))
    s5_shape = state_s5_re.shape[2:]

    x_meta = jnp.broadcast_to(meta_tokens[None].astype(dt), (bp,) + meta_tokens.shape)
    _, meta_states = _trunk(x_meta, None, p, alpha=alpha)
    y_prompt, new_p = _trunk(x_prompt, _carried(meta_states), p, alpha=alpha)
    y_sample, new_s = _trunk(x_sample, _given(state_s5_re, state_s5_im, state_rg_h, state_rg_conv,
                                              state_ffn_conv), p, alpha=alpha)
    return (y_prompt, y_sample) + _stack_states(new_p, s5_shape) + _stack_states(new_s, s5_shape)
```

```python
import functools
import math

import jax
import jax.numpy as jnp
from jax import lax
from jax.experimental import pallas as pl
from jax.experimental.pallas import tpu as pltpu

RG_BLOCKS = 4
RG_C = 8.0
LN_EPS = 1e-5

SUBLANES = 8
S5_CHUNK_GROUPS = 8
MXU_COLS = 256
FF_TILE = MXU_COLS
BLOCK_ROWS = 1024
SUB_ROWS = 512
VMEM_LIMIT_BYTES = 56 * 1024 * 1024

BF16 = jnp.bfloat16
F32 = jnp.float32


def _dot(a, b):
    return jnp.dot(a, b, preferred_element_type=F32)


def _layer_norm(z, g, b):
    mu = jnp.mean(z, axis=-1, keepdims=True)
    zc = z - mu
    var = jnp.mean(zc * zc, axis=-1, keepdims=True)
    return zc * lax.rsqrt(var + LN_EPS) * g + b


def _largest_divisor(n, limit):
    return max(d for d in range(1, n + 1) if n % d == 0 and d <= limit)


def _tiling(seq_len, batch, min_steps):
    steps = _largest_divisor(seq_len, max(BLOCK_ROWS // batch, min_steps))
    sub = _largest_divisor(steps, max(SUB_ROWS // batch, min_steps))
    assert sub >= min_steps, (seq_len, batch, min_steps)
    return steps, sub


def _whole(a):
    return a, pl.BlockSpec(a.shape, lambda i: (0,) * a.ndim, pipeline_mode=pl.Buffered(1))


def _pick(a, fixed):
    block = tuple(None if d in fixed else n for d, n in enumerate(a.shape))
    index = tuple(fixed.get(d, 0) for d in range(a.ndim))
    return a, pl.BlockSpec(block, lambda i: index, pipeline_mode=pl.Buffered(1))


def _row_specs(batch, steps, d_model, batch_major):
    if batch_major:
        return pl.BlockSpec((batch, steps, d_model), lambda i: (0, i, 0))
    return pl.BlockSpec((steps * batch, d_model), lambda i: (i, 0))


def _load_rows(x_ref, t0, n_steps, batch, batch_major, order=None):
    if batch_major:
        return jnp.concatenate([x_ref[:, t0 + t, :] for t in (order or range(n_steps))], axis=0)
    if order is None:
        return x_ref[t0 * batch:(t0 + n_steps) * batch, :]
    return jnp.concatenate([x_ref[(t0 + t) * batch:(t0 + t + 1) * batch, :] for t in order], axis=0)


def _store_rows(out_ref, t0, n_steps, batch, batch_major, val, order=None):
    if batch_major or order is not None:
        for i, t in enumerate(order or range(n_steps)):
            if batch_major:
                out_ref[:, t0 + t, :] = val[i * batch:(i + 1) * batch, :]
            else:
                out_ref[(t0 + t) * batch:(t0 + t + 1) * batch, :] = val[i * batch:(i + 1) * batch, :]
    else:
        out_ref[t0 * batch:(t0 + n_steps) * batch, :] = val


def _init_carry(carry_refs, init_refs):
    @pl.when(pl.program_id(0) == 0)
    def _():
        for k, ref in enumerate(carry_refs):
            ref[...] = init_refs[k][...] if init_refs else jnp.zeros_like(ref)


def _slab_scan(step_fn, carry, n_steps, batch):
    n_groups = batch // SUBLANES
    outs = [[None] * n_groups for _ in range(n_steps)]
    carry = list(carry)
    for t in range(n_steps):
        for rg in range(n_groups):
            row = t * batch + rg * SUBLANES
            carry[rg], outs[t][rg] = step_fn(carry[rg], row)
    n_out = len(outs[0][0])
    stacked = [jnp.concatenate([outs[t][rg][k] for t in range(n_steps) for rg in range(n_groups)], axis=0)
               for k in range(n_out)]
    return carry, stacked


def _causal_conv(up, hist_refs, col, cw, cb, batch):
    rows, width = up.shape
    n_hist = len(hist_refs)
    ext = jnp.concatenate([r[:, col:col + width] for r in hist_refs] + [up], axis=0)
    for k in range(n_hist):
        hist_refs[k][:, col:col + width] = ext[rows + k * batch:rows + (k + 1) * batch]
    y = cb + cw[0] * ext[0:rows]
    for k in range(1, n_hist + 1):
        y = y + cw[k] * ext[k * batch:k * batch + rows]
    return y


def _call(kern, name, x, x_spec, operands, out_shapes, out_specs, scratch=()):
    arrays, specs = zip(*operands)
    return pl.pallas_call(
        kern,
        grid=(x_spec_grid(x, x_spec),),
        in_specs=[x_spec, *specs],
        out_specs=out_specs,
        out_shape=out_shapes,
        scratch_shapes=list(scratch),
        compiler_params=pltpu.CompilerParams(
            dimension_semantics=("arbitrary",), vmem_limit_bytes=VMEM_LIMIT_BYTES),
        name=name,
    )(x, *arrays)


def x_spec_grid(x, x_spec):
    dim = 1 if x.ndim == 3 else 0
    return x.shape[dim] // x_spec.block_shape[dim]


def _carry_spec(shape):
    return pl.BlockSpec(shape, lambda i: (0,) * len(shape))


def _s5_kernel(*refs, layer, ln_row, batch, steps, sub_steps, alpha, n_init, x_batch_major):
    x_ref, refs = refs[0], refs[1:]
    init_refs, refs = refs[:n_init], refs[n_init:]
    (win_ref, drive_ref, read_ref, a2re_ref, a2im_ref, d_ref, wout_ref, g_ref, b_ref,
     out_ref, sre_ref, sim_ref) = refs
    d_model = out_ref.shape[1]
    n_chunks, cin2, cst2 = drive_ref.shape
    cin = cin2 // 2
    cst = cst2 // 2
    n_pairs = sub_steps // 2
    pair_rows = n_pairs * batch
    n_groups = batch // SUBLANES
    order = list(range(0, sub_steps, 2)) + list(range(1, sub_steps, 2))
    _init_carry((sre_ref, sim_ref), init_refs)
    ln_g, ln_b = g_ref[ln_row[0], ln_row[1]:ln_row[1] + 1, :], b_ref[ln_row[0], ln_row[1]:ln_row[1] + 1, :]
    d_skip = d_ref[layer:layer + 1, :]

    def rows_of(ref, rg, c):
        return ref[rg * SUBLANES:(rg + 1) * SUBLANES, c * cst:(c + 1) * cst]

    state = [[(rows_of(sre_ref, rg, c), rows_of(sim_ref, rg, c)) for rg in range(n_groups)]
             for c in range(n_chunks)]
    def pair_inputs(u, c):
        cols = slice(c * cin, (c + 1) * cin)
        return jnp.concatenate([u[:pair_rows, cols], u[pair_rows:, cols]], axis=1).astype(BF16)

    def scan_readout(up, drv, c):
        a_re = jnp.broadcast_to(a2re_ref[c:c + 1, :], (SUBLANES, cst))
        a_im = jnp.broadcast_to(a2im_ref[c:c + 1, :], (SUBLANES, cst))

        def step(carry, row):
            s_re, s_im = carry
            n_re = a_re * s_re - a_im * s_im + drv[row:row + SUBLANES, 0:cst]
            n_im = a_re * s_im + a_im * s_re + drv[row:row + SUBLANES, cst:cst2]
            return (n_re, n_im), (s_re, s_im)

        state[c], (st_re, st_im) = _slab_scan(step, state[c], n_pairs, batch)
        lhs = jnp.concatenate([st_re.astype(BF16), st_im.astype(BF16), up], axis=1)
        return _dot(lhs, read_ref[c])

    def finish(s, x, u, ys):
        y = jnp.concatenate([jnp.concatenate([yp[:, :cin] for yp in ys], axis=1),
                             jnp.concatenate([yp[:, cin:] for yp in ys], axis=1)], axis=0) + d_skip * u
        vg = _dot(jax.nn.gelu(y).astype(BF16), wout_ref[...])
        mix = vg[:, :d_model] * jax.nn.sigmoid(vg[:, d_model:])
        _store_rows(out_ref, s * sub_steps, sub_steps, batch, False,
                    _layer_norm(alpha * x + mix, ln_g, ln_b), order)

    pending = None
    for s in range(steps // sub_steps):
        x = _load_rows(x_ref, s * sub_steps, sub_steps, batch, x_batch_major, order)
        u = _dot(x.astype(BF16), win_ref[...])
        ys = []
        up_next = pair_inputs(u, 0)
        drv_next = _dot(up_next, drive_ref[0])
        for c in range(n_chunks):
            up, drv = up_next, drv_next
            if c + 1 < n_chunks:
                up_next = pair_inputs(u, c + 1)
                drv_next = _dot(up_next, drive_ref[c + 1])
            ys.append(scan_readout(up, drv, c))
            if c == 0 and pending is not None:
                finish(*pending)
        pending = (s, x, u, ys)
    finish(*pending)
    for c in range(n_chunks):
        for rg in range(n_groups):
            sre_ref[rg * SUBLANES:(rg + 1) * SUBLANES, c * cst:(c + 1) * cst] = state[c][rg][0]
            sim_ref[rg * SUBLANES:(rg + 1) * SUBLANES, c * cst:(c + 1) * cst] = state[c][rg][1]


def _s5_layer(h, init, p, j, ln_row, *, batch, seq_len, alpha):
    batch_major = h.ndim == 3
    d_model = h.shape[-1]
    steps, sub_steps = _tiling(seq_len, batch, 2)
    assert sub_steps % 2 == 0, (seq_len, batch)
    n_state = p["a2_re"].shape[1] * p["a2_re"].shape[2]
    operands = list(init) + [
        _pick(p["s5_w_in"], {0: j}), _pick(p["s5_drive"], {0: j}), _pick(p["s5_read"], {0: j}),
        _pick(p["a2_re"], {0: j}), _pick(p["a2_im"], {0: j}), _whole(p["s5_d"]),
        _pick(p["s5_w_out"], {0: j}), _whole(p["ln_g"]), _whole(p["ln_b"])]
    kern = functools.partial(_s5_kernel, layer=j, ln_row=ln_row, batch=batch, steps=steps,
                             sub_steps=sub_steps, alpha=alpha, n_init=len(init), x_batch_major=batch_major)
    return _call(
        kern, "s5_mixer_ln", h, _row_specs(batch, steps, d_model, batch_major), operands,
        [jax.ShapeDtypeStruct((seq_len * batch, d_model), F32),
         jax.ShapeDtypeStruct((batch, n_state), F32), jax.ShapeDtypeStruct((batch, n_state), F32)],
        [_row_specs(batch, steps, d_model, False), _carry_spec((batch, n_state)), _carry_spec((batch, n_state))])


def _s5_params(lam_re, lam_im, log_step, b_re, b_im, c_re, c_im):
    n_layers, n_groups, n_state, gsz = b_re.shape
    gc = S5_CHUNK_GROUPS
    nc = n_groups // gc
    exact = lax.Precision.HIGHEST
    step = jnp.exp(log_step.astype(F32))[..., None]
    lr, li = lam_re.astype(F32), lam_im.astype(F32)
    mag = jnp.exp(lr * step)
    ab_re, ab_im = mag * jnp.cos(li * step), mag * jnp.sin(li * step)
    den = lr * lr + li * li
    nr, ni = ab_re - 1.0, ab_im
    q_re = ((nr * lr + ni * li) / den)[..., None]
    q_im = ((ni * lr - nr * li) / den)[..., None]
    br, bi = b_re.astype(F32), b_im.astype(F32)
    bb_re = q_re * br - q_im * bi
    bb_im = q_re * bi + q_im * br
    a2_re, a2_im = ab_re * ab_re - ab_im * ab_im, 2.0 * ab_re * ab_im
    lb_re = ab_re[..., None] * bb_re - ab_im[..., None] * bb_im
    lb_im = ab_re[..., None] * bb_im + ab_im[..., None] * bb_re
    cr, ci = c_re.astype(F32), c_im.astype(F32)

    def c_times(m_re, m_im):
        m_re, m_im = m_re[:, :, None, :], m_im[:, :, None, :]
        return cr * m_re - ci * m_im, cr * m_im + ci * m_re

    def re_c_times(m_re, m_im):
        return (jnp.einsum("ngop,ngpi->ngio", cr, m_re, precision=exact)
                - jnp.einsum("ngop,ngpi->ngio", ci, m_im, precision=exact))

    eye = jnp.eye(gc, dtype=F32)[None, None, :, :, None, None]

    def blocks(t, transpose):
        a, b = t.shape[2], t.shape[3]
        t = t.reshape(n_layers, nc, gc, 1, a, b) * eye
        if transpose:
            return t.transpose(0, 1, 2, 5, 3, 4).reshape(n_layers, nc, gc * b, gc * a)
        return t.transpose(0, 1, 2, 4, 3, 5).reshape(n_layers, nc, gc * a, gc * b)

    drive = jnp.concatenate([
        jnp.concatenate([blocks(lb_re, True), blocks(lb_im, True)], axis=3),
        jnp.concatenate([blocks(bb_re, True), blocks(bb_im, True)], axis=3)], axis=2).astype(BF16)
    cl_re, cl_im = c_times(ab_re, ab_im)
    cl2_re, cl2_im = c_times(a2_re, a2_im)
    k0 = blocks(re_c_times(bb_re, bb_im), False)
    k1 = blocks(re_c_times(lb_re, lb_im), False)
    read = jnp.concatenate([
        jnp.concatenate([blocks(cl_re, True), blocks(cl2_re, True)], axis=3),
        jnp.concatenate([-blocks(cl_im, True), -blocks(cl2_im, True)], axis=3),
        jnp.concatenate([k0, k1], axis=3),
        jnp.concatenate([jnp.zeros_like(k0), k0], axis=3)], axis=2).astype(BF16)
    return dict(s5_drive=drive, s5_read=read, a2_re=a2_re.reshape(n_layers, nc, gc * n_state),
                a2_im=a2_im.reshape(n_layers, nc, gc * n_state))


def _softplus(z):
    return jnp.maximum(z, 0.0) + jnp.log1p(jnp.exp(-jnp.abs(z)))


def _rg_kernel(*refs, layer, ln_row, batch, steps, sub_steps, alpha, n_init):
    x_ref, refs = refs[0], refs[1:]
    init_refs, refs = refs[:n_init], refs[n_init:]
    (win_ref, cw_ref, cb_ref, wg_ref, bg_ref, lam_ref, wout_ref, g_ref, b_ref, out_ref, hn_ref) = refs[:11]
    hist_refs = refs[11:]
    width = hn_ref.shape[1]
    blk = width // RG_BLOCKS
    sub_rows = sub_steps * batch
    n_groups = batch // SUBLANES
    _init_carry((hn_ref, *hist_refs), init_refs)
    ln_g, ln_b = g_ref[ln_row[0], ln_row[1]:ln_row[1] + 1, :], b_ref[ln_row[0], ln_row[1]:ln_row[1] + 1, :]
    cw = [cw_ref[layer, k:k + 1, :] for k in range(cw_ref.shape[1])]
    cb = cb_ref[layer:layer + 1, :]
    b_r, b_i = bg_ref[layer:layer + 1, :width], bg_ref[layer:layer + 1, width:]

    c_softplus = RG_C * _softplus(-lam_ref[layer:layer + 1, :])
    h_state = [[hn_ref[rg * SUBLANES:(rg + 1) * SUBLANES, n * blk:(n + 1) * blk] for rg in range(n_groups)]
               for n in range(RG_BLOCKS)]

    def gate_block(xb, n):
        cols = slice(n * blk, (n + 1) * blk)
        zx = _dot(xb, win_ref[:, width + n * blk:width + (n + 1) * blk])
        xc = _causal_conv(zx, hist_refs, n * blk, [w[:, cols] for w in cw], cb[:, cols], batch)
        gts = _dot(xc.astype(BF16), wg_ref[n])
        r = jax.nn.sigmoid(gts[:, :blk] + b_r[:, cols])
        ig = jax.nn.sigmoid(gts[:, blk:] + b_i[:, cols])
        neg_log_a = r * c_softplus[:, cols]
        a = jnp.exp(-neg_log_a)
        th = jnp.tanh(neg_log_a)
        v = th / (1.0 + th)
        root = jnp.where(v > 0.0, v * lax.rsqrt(v), 0.0)
        bb = (math.sqrt(2.0) * root) * (ig * xc)

        def step(h, row):
            h = a[row:row + SUBLANES, :] * h + bb[row:row + SUBLANES, :]
            return h, (h,)

        h_state[n], (hs,) = _slab_scan(step, h_state[n], sub_steps, batch)
        gate = jax.nn.gelu(_dot(xb, win_ref[:, cols]))
        return (hs * gate).astype(BF16)

    def finish(s, x, hg):
        y = _dot(hg, wout_ref[...])
        out_ref[s * sub_rows:(s + 1) * sub_rows, :] = _layer_norm(alpha * x + y, ln_g, ln_b)

    pending = None
    for s in range(steps // sub_steps):
        x = x_ref[s * sub_rows:(s + 1) * sub_rows, :]
        xb = x.astype(BF16)
        blocks = []
        for n in range(RG_BLOCKS):
            blocks.append(gate_block(xb, n))
            if n == 0 and pending is not None:
                finish(*pending)
        pending = (s, x, jnp.concatenate(blocks, axis=1))
    finish(*pending)
    for n in range(RG_BLOCKS):
        for rg in range(n_groups):
            hn_ref[rg * SUBLANES:(rg + 1) * SUBLANES, n * blk:(n + 1) * blk] = h_state[n][rg]


def _rg_layer(h, init, p, j, ln_row, *, batch, seq_len, alpha):
    rows, d_model = h.shape
    width = p["rg_w_out"].shape[1]
    n_taps = p["rg_conv_w"].shape[1]
    steps, sub_steps = _tiling(seq_len, batch, n_taps - 1)
    operands = list(init) + [
        _pick(p["rg_w_in"], {0: j}), _whole(p["rg_conv_w"]), _whole(p["rg_conv_b"]),
        _pick(p["rg_w_gates"], {0: j}), _whole(p["rg_b_gates"]), _whole(p["rg_lam"]),
        _pick(p["rg_w_out"], {0: j}), _whole(p["ln_g"]), _whole(p["ln_b"])]
    kern = functools.partial(_rg_kernel, layer=j, ln_row=ln_row, batch=batch, steps=steps,
                             sub_steps=sub_steps, alpha=alpha, n_init=len(init))
    n_carry = n_taps
    return _call(
        kern, "rglru_mixer_ln", h, _row_specs(batch, steps, d_model, False), operands,
        [jax.ShapeDtypeStruct((rows, d_model), F32)]
        + [jax.ShapeDtypeStruct((batch, width), F32)] * n_carry,
        [_row_specs(batch, steps, d_model, False)] + [_carry_spec((batch, width))] * n_carry)


def _ffn_kernel(*refs, layer, ln_row, batch, steps, sub_steps, alpha, n_init, out_batch_major):
    x_ref, refs = refs[0], refs[1:]
    init_refs, refs = refs[:n_init], refs[n_init:]
    wup_ref, cw_ref, cb_ref, wd_ref, g_ref, b_ref, out_ref = refs[:7]
    hist_refs, act_scr = refs[7:-1], refs[-1]
    ff = wd_ref.shape[0]
    n_taps = cw_ref.shape[1]
    sub_rows = sub_steps * batch
    _init_carry(hist_refs, init_refs)
    ln_g, ln_b = g_ref[ln_row[0], ln_row[1]:ln_row[1] + 1, :], b_ref[ln_row[0], ln_row[1]:ln_row[1] + 1, :]

    def finish(s, x):
        y = _dot(act_scr[s % 2], wd_ref[...])
        _store_rows(out_ref, s * sub_steps, sub_steps, batch, out_batch_major,
                    _layer_norm(alpha * x + y, ln_g, ln_b))

    pending = None
    for s in range(steps // sub_steps):
        x = x_ref[s * sub_rows:(s + 1) * sub_rows, :]
        xb = x.astype(BF16)
        for j in range(ff // FF_TILE):
            val, gate = [
                _causal_conv(_dot(xb, wup_ref[:, col:col + FF_TILE]), hist_refs, col,
                             [cw_ref[layer, k:k + 1, col:col + FF_TILE] for k in range(n_taps)],
                             cb_ref[layer:layer + 1, col:col + FF_TILE], batch)
                for col in (j * FF_TILE, ff + j * FF_TILE)]
            act_scr[s % 2, :, j * FF_TILE:(j + 1) * FF_TILE] = (jax.nn.gelu(gate) * val).astype(BF16)
            if j == 0 and pending is not None:
                finish(*pending)
        pending = (s, x)
    finish(*pending)


def _ffn_layer(h, init, p, i, ln_row, *, batch, seq_len, alpha, out_batch_major):
    rows, d_model = h.shape
    ff = p["ffn_w_down"].shape[1]
    n_taps = p["ffn_conv_w"].shape[1]
    steps, sub_steps = _tiling(seq_len, batch, n_taps - 1)
    operands = list(init) + [
        _pick(p["ffn_w_up"], {0: i}), _whole(p["ffn_conv_w"]), _whole(p["ffn_conv_b"]),
        _pick(p["ffn_w_down"], {0: i}), _whole(p["ln_g"]), _whole(p["ln_b"])]
    kern = functools.partial(_ffn_kernel, layer=i, ln_row=ln_row, batch=batch, steps=steps,
                             sub_steps=sub_steps, alpha=alpha, n_init=len(init),
                             out_batch_major=out_batch_major)
    out_shape = (batch, seq_len, d_model) if out_batch_major else (rows, d_model)
    return _call(
        kern, "convffn_ln", h, _row_specs(batch, steps, d_model, False), operands,
        [jax.ShapeDtypeStruct(out_shape, F32)] + [jax.ShapeDtypeStruct((batch, 2 * ff), F32)] * (n_taps - 1),
        [_row_specs(batch, steps, d_model, out_batch_major)] + [_carry_spec((batch, 2 * ff))] * (n_taps - 1),
        scratch=[pltpu.VMEM((2, sub_steps * batch, ff), BF16)])


def _trunk(x, init, p, *, alpha):
    batch, seq_len, _ = x.shape
    depth = p["ffn_w_up"].shape[0]
    new = dict(s5_re=[], s5_im=[], rg_h=[], rg_conv=[], ffn_conv=[])
    kw = dict(batch=batch, seq_len=seq_len, alpha=alpha)
    h = x
    for i in range(depth):
        j = i // 2
        if i % 2 == 0:
            st = () if init is None else (init["s5_re"][j], init["s5_im"][j])
            h, sr, si = _s5_layer(h, st, p, j, (i, 0), **kw)
            new["s5_re"].append(sr)
            new["s5_im"].append(si)
        else:
            st = () if init is None else (init["rg_h"][j], *init["rg_conv"][j])
            h, hn, *hist = _rg_layer(h, st, p, j, (i, 0), **kw)
            new["rg_h"].append(hn)
            new["rg_conv"].append(hist)
        st = () if init is None else tuple(init["ffn_conv"][i])
        h, *hist = _ffn_layer(h, st, p, i, (i, 1), out_batch_major=(i == depth - 1), **kw)
        new["ffn_conv"].append(hist)
    return h, new


def _carried(new):
    one = lambda a: _whole(a)
    return dict(s5_re=[one(a) for a in new["s5_re"]], s5_im=[one(a) for a in new["s5_im"]],
                rg_h=[one(a) for a in new["rg_h"]],
                rg_conv=[[one(a) for a in hist] for hist in new["rg_conv"]],
                ffn_conv=[[one(a) for a in hist] for hist in new["ffn_conv"]])


def _given(s5_re, s5_im, rg_h, rg_conv, ffn_conv):
    flat = lambda a: [_whole(a[k].reshape(a.shape[1], -1)) for k in range(a.shape[0])]

    def steps(a):
        t = a.transpose(0, 2, 1, 3)
        return [[_pick(t, {0: n, 1: k}) for k in range(t.shape[1])] for n in range(t.shape[0])]

    return dict(s5_re=flat(s5_re), s5_im=flat(s5_im), rg_h=[_pick(rg_h, {0: n}) for n in range(rg_h.shape[0])],
                rg_conv=steps(rg_conv), ffn_conv=steps(ffn_conv))


def _stack_states(new, s5_shape):
    s5 = lambda k: jnp.stack(new[k]).reshape((len(new[k]), new[k][0].shape[0]) + s5_shape)
    conv = lambda k: jnp.stack([jnp.stack(hist, axis=1) for hist in new[k]])
    return (s5("s5_re"), s5("s5_im"), jnp.stack(new["rg_h"]), conv("rg_conv"), conv("ffn_conv"))


def kernel(x_prompt, x_sample, state_s5_re, state_s5_im, state_rg_h, state_rg_conv, state_ffn_conv,
           meta_tokens, s5_w_in, s5_lam_re, s5_lam_im, s5_log_step, s5_b_re, s5_b_im, s5_c_re,
           s5_c_im, s5_d, s5_w_out, rg_w_in, rg_conv_w, rg_conv_b, rg_w_gates, rg_b_gates, rg_lam,
           rg_w_out, ffn_w_up, ffn_conv_w, ffn_conv_b, ffn_w_down, ln_g, ln_b):
    depth = ffn_w_up.shape[0]
    alpha = (2 * depth) ** 0.25
    bp = x_prompt.shape[0]
    dt = x_prompt.dtype

    p = dict(
        s5_w_in=s5_w_in.astype(BF16), s5_w_out=s5_w_out.astype(BF16), s5_d=s5_d,
        rg_w_in=rg_w_in.astype(BF16), rg_conv_w=rg_conv_w, rg_conv_b=rg_conv_b,
        rg_w_gates=rg_w_gates.astype(BF16), rg_b_gates=rg_b_gates, rg_lam=rg_lam,
        rg_w_out=rg_w_out.astype(BF16),
        ffn_w_up=ffn_w_up.astype(BF16), ffn_conv_w=ffn_conv_w, ffn_conv_b=ffn_conv_b,
        ffn_w_down=ffn_w_down.astype(BF16), ln_g=ln_g, ln_b=ln_b,
        **_s5_params(s5_lam_re, s5_lam_im, s5_log_step, s5_b_re, s5_b_im, s5_c_re, s5_c_im))
    s5_shape = state_s5_re.shape[2:]

    x_meta = jnp.broadcast_to(meta_tokens[None].astype(dt), (bp,) + meta_tokens.shape)
    _, meta_states = _trunk(x_meta, None, p, alpha=alpha)
    y_prompt, new_p = _trunk(x_prompt, _carried(meta_states), p, alpha=alpha)
    y_sample, new_s = _trunk(x_sample, _given(state_s5_re, state_s5_im, state_rg_h, state_rg_conv,
                                              state_ffn_conv), p, alpha=alpha)
    return (y_prompt, y_sample) + _stack_states(new_p, s5_shape) + _stack_states(new_s, s5_shape)
```

```python
import functools
import math

import jax
import jax.numpy as jnp
import numpy as np
from jax import lax
from jax.experimental import pallas as pl
from jax.experimental.pallas import tpu as pltpu

RG_BLOCKS = 4
RG_C = 8.0
LN_EPS = 1e-5

SUBLANES = 8
S5_CHUNK_GROUPS = 8
MXU_COLS = 256
FF_TILE = MXU_COLS
BLOCK_ROWS = 1024
SUB_ROWS = 512
VMEM_LIMIT_BYTES = 56 * 1024 * 1024

BF16 = jnp.bfloat16
F32 = jnp.float32


def _dot(a, b):
    return jnp.dot(a, b, preferred_element_type=F32)


def _layer_norm(z, g, b):
    mu = jnp.mean(z, axis=-1, keepdims=True)
    zc = z - mu
    var = jnp.mean(zc * zc, axis=-1, keepdims=True)
    return zc * lax.rsqrt(var + LN_EPS) * g + b


def _largest_divisor(n, limit):
    return max(d for d in range(1, n + 1) if n % d == 0 and d <= limit)


def _tiling(seq_len, batch, min_steps):
    steps = _largest_divisor(seq_len, max(BLOCK_ROWS // batch, min_steps))
    sub = _largest_divisor(steps, max(SUB_ROWS // batch, min_steps))
    assert sub >= min_steps, (seq_len, batch, min_steps)
    return steps, sub


def _whole(a):
    return a, pl.BlockSpec(a.shape, lambda i: (0,) * a.ndim, pipeline_mode=pl.Buffered(1))


def _pick(a, fixed):
    block = tuple(None if d in fixed else n for d, n in enumerate(a.shape))
    index = tuple(fixed.get(d, 0) for d in range(a.ndim))
    return a, pl.BlockSpec(block, lambda i: index, pipeline_mode=pl.Buffered(1))


def _row_specs(batch, steps, d_model, batch_major):
    if batch_major:
        return pl.BlockSpec((batch, steps, d_model), lambda i: (0, i, 0))
    return pl.BlockSpec((steps * batch, d_model), lambda i: (i, 0))


def _load_rows(x_ref, t0, n_steps, batch, batch_major, order=None):
    if batch_major:
        return jnp.concatenate([x_ref[:, t0 + t, :] for t in (order or range(n_steps))], axis=0)
    if order is None:
        return x_ref[t0 * batch:(t0 + n_steps) * batch, :]
    return jnp.concatenate([x_ref[(t0 + t) * batch:(t0 + t + 1) * batch, :] for t in order], axis=0)


def _store_rows(out_ref, t0, n_steps, batch, batch_major, val, order=None):
    if batch_major or order is not None:
        for i, t in enumerate(order or range(n_steps)):
            if batch_major:
                out_ref[:, t0 + t, :] = val[i * batch:(i + 1) * batch, :]
            else:
                out_ref[(t0 + t) * batch:(t0 + t + 1) * batch, :] = val[i * batch:(i + 1) * batch, :]
    else:
        out_ref[t0 * batch:(t0 + n_steps) * batch, :] = val


def _init_carry(carry_refs, init_refs):
    @pl.when(pl.program_id(0) == 0)
    def _():
        for k, ref in enumerate(carry_refs):
            ref[...] = init_refs[k][...] if init_refs else jnp.zeros_like(ref)


def _slab_scan(step_fn, carry, n_steps, batch):
    n_groups = batch // SUBLANES
    outs = [[None] * n_groups for _ in range(n_steps)]
    carry = list(carry)
    for t in range(n_steps):
        for rg in range(n_groups):
            row = t * batch + rg * SUBLANES
            carry[rg], outs[t][rg] = step_fn(carry[rg], row)
    n_out = len(outs[0][0])
    stacked = [jnp.concatenate([outs[t][rg][k] for t in range(n_steps) for rg in range(n_groups)], axis=0)
               for k in range(n_out)]
    return carry, stacked


def _causal_conv(up, hist_refs, col, cw, cb, batch):
    rows, width = up.shape
    n_hist = len(hist_refs)
    ext = jnp.concatenate([r[:, col:col + width] for r in hist_refs] + [up], axis=0)
    for k in range(n_hist):
        hist_refs[k][:, col:col + width] = ext[rows + k * batch:rows + (k + 1) * batch]
    y = cb + cw[0] * ext[0:rows]
    for k in range(1, n_hist + 1):
        y = y + cw[k] * ext[k * batch:k * batch + rows]
    return y


def _call(kern, name, x, x_spec, operands, out_shapes, out_specs, scratch=()):
    arrays, specs = zip(*operands)
    return pl.pallas_call(
        kern,
        grid=(x_spec_grid(x, x_spec),),
        in_specs=[x_spec, *specs],
        out_specs=out_specs,
        out_shape=out_shapes,
        scratch_shapes=list(scratch),
        compiler_params=pltpu.CompilerParams(
            dimension_semantics=("arbitrary",), vmem_limit_bytes=VMEM_LIMIT_BYTES),
        name=name,
    )(x, *arrays)


def x_spec_grid(x, x_spec):
    dim = 1 if x.ndim == 3 else 0
    return x.shape[dim] // x_spec.block_shape[dim]


def _carry_spec(shape):
    return pl.BlockSpec(shape, lambda i: (0,) * len(shape))


def _s5_kernel(*refs, layer, ln_row, batch, steps, sub_steps, alpha, n_init, x_batch_major):
    x_ref, refs = refs[0], refs[1:]
    init_refs, refs = refs[:n_init], refs[n_init:]
    (win_ref, drive_ref, read_ref, a2re_ref, a2im_ref, d_ref, wout_ref, g_ref, b_ref,
     out_ref, sre_ref, sim_ref) = refs
    d_model = out_ref.shape[1]
    n_chunks, cin2, cst2 = drive_ref.shape
    cin = cin2 // 2
    cst = cst2 // 2
    n_pairs = sub_steps // 2
    pair_rows = n_pairs * batch
    n_groups = batch // SUBLANES
    order = list(range(0, sub_steps, 2)) + list(range(1, sub_steps, 2))
    _init_carry((sre_ref, sim_ref), init_refs)
    ln_g, ln_b = g_ref[ln_row[0], ln_row[1]:ln_row[1] + 1, :], b_ref[ln_row[0], ln_row[1]:ln_row[1] + 1, :]
    d_skip = d_ref[layer:layer + 1, :]

    def rows_of(ref, rg, c):
        return ref[rg * SUBLANES:(rg + 1) * SUBLANES, c * cst:(c + 1) * cst]

    state = [[(rows_of(sre_ref, rg, c), rows_of(sim_ref, rg, c)) for rg in range(n_groups)]
             for c in range(n_chunks)]
    def pair_inputs(u, c):
        cols = slice(c * cin, (c + 1) * cin)
        return jnp.concatenate([u[:pair_rows, cols], u[pair_rows:, cols]], axis=1).astype(BF16)

    def scan_readout(up, drv, c):
        a_re = jnp.broadcast_to(a2re_ref[c:c + 1, :], (SUBLANES, cst))
        a_im = jnp.broadcast_to(a2im_ref[c:c + 1, :], (SUBLANES, cst))

        def step(carry, row):
            s_re, s_im = carry
            n_re = a_re * s_re - a_im * s_im + drv[row:row + SUBLANES, 0:cst]
            n_im = a_re * s_im + a_im * s_re + drv[row:row + SUBLANES, cst:cst2]
            return (n_re, n_im), (s_re, s_im)

        state[c], (st_re, st_im) = _slab_scan(step, state[c], n_pairs, batch)
        lhs = jnp.concatenate([st_re.astype(BF16), st_im.astype(BF16), up], axis=1)
        return _dot(lhs, read_ref[c])

    def finish(s, x, u, ys):
        y = jnp.concatenate([jnp.concatenate([yp[:, :cin] for yp in ys], axis=1),
                             jnp.concatenate([yp[:, cin:] for yp in ys], axis=1)], axis=0) + d_skip * u
        vg = _dot(jax.nn.gelu(y).astype(BF16), wout_ref[...])
        mix = vg[:, :d_model] * jax.nn.sigmoid(vg[:, d_model:])
        _store_rows(out_ref, s * sub_steps, sub_steps, batch, False,
                    _layer_norm(alpha * x + mix, ln_g, ln_b), order)

    pending = None
    for s in range(steps // sub_steps):
        x = _load_rows(x_ref, s * sub_steps, sub_steps, batch, x_batch_major, order)
        u = _dot(x.astype(BF16), win_ref[...])
        ys = []
        up_next = pair_inputs(u, 0)
        drv_next = _dot(up_next, drive_ref[0])
        for c in range(n_chunks):
            up, drv = up_next, drv_next
            if c + 1 < n_chunks:
                up_next = pair_inputs(u, c + 1)
                drv_next = _dot(up_next, drive_ref[c + 1])
            ys.append(scan_readout(up, drv, c))
            if c == 0 and pending is not None:
                finish(*pending)
        pending = (s, x, u, ys)
    finish(*pending)
    for c in range(n_chunks):
        for rg in range(n_groups):
            sre_ref[rg * SUBLANES:(rg + 1) * SUBLANES, c * cst:(c + 1) * cst] = state[c][rg][0]
            sim_ref[rg * SUBLANES:(rg + 1) * SUBLANES, c * cst:(c + 1) * cst] = state[c][rg][1]


def _s5_layer(h, init, p, j, ln_row, *, batch, seq_len, alpha):
    batch_major = h.ndim == 3
    d_model = h.shape[-1]
    steps, sub_steps = _tiling(seq_len, batch, 2)
    assert sub_steps % 2 == 0, (seq_len, batch)
    n_state = p["a2_re"].shape[1] * p["a2_re"].shape[2]
    operands = list(init) + [
        _pick(p["s5_w_in"], {0: j}), _pick(p["s5_drive"], {0: j}), _pick(p["s5_read"], {0: j}),
        _pick(p["a2_re"], {0: j}), _pick(p["a2_im"], {0: j}), _whole(p["s5_d"]),
        _pick(p["s5_w_out"], {0: j}), _whole(p["ln_g"]), _whole(p["ln_b"])]
    kern = functools.partial(_s5_kernel, layer=j, ln_row=ln_row, batch=batch, steps=steps,
                             sub_steps=sub_steps, alpha=alpha, n_init=len(init), x_batch_major=batch_major)
    return _call(
        kern, "s5_mixer_ln", h, _row_specs(batch, steps, d_model, batch_major), operands,
        [jax.ShapeDtypeStruct((seq_len * batch, d_model), F32),
         jax.ShapeDtypeStruct((batch, n_state), F32), jax.ShapeDtypeStruct((batch, n_state), F32)],
        [_row_specs(batch, steps, d_model, False), _carry_spec((batch, n_state)), _carry_spec((batch, n_state))])


def _s5_params(lam_re, lam_im, log_step, b_re, b_im, c_re, c_im):
    n_layers, n_groups, n_state, gsz = b_re.shape
    gc = S5_CHUNK_GROUPS
    nc = n_groups // gc
    exact = lax.Precision.HIGHEST
    step = jnp.exp(log_step.astype(F32))[..., None]
    lr, li = lam_re.astype(F32), lam_im.astype(F32)
    mag = jnp.exp(lr * step)
    ab_re, ab_im = mag * jnp.cos(li * step), mag * jnp.sin(li * step)
    den = lr * lr + li * li
    nr, ni = ab_re - 1.0, ab_im
    q_re = ((nr * lr + ni * li) / den)[..., None]
    q_im = ((ni * lr - nr * li) / den)[..., None]
    br, bi = b_re.astype(F32), b_im.astype(F32)
    bb_re = q_re * br - q_im * bi
    bb_im = q_re * bi + q_im * br
    a2_re, a2_im = ab_re * ab_re - ab_im * ab_im, 2.0 * ab_re * ab_im
    lb_re = ab_re[..., None] * bb_re - ab_im[..., None] * bb_im
    lb_im = ab_re[..., None] * bb_im + ab_im[..., None] * bb_re
    cr, ci = c_re.astype(F32), c_im.astype(F32)

    def c_times(m_re, m_im):
        m_re, m_im = m_re[:, :, None, :], m_im[:, :, None, :]
        return cr * m_re - ci * m_im, cr * m_im + ci * m_re

    def re_c_times(m_re, m_im):
        return (jnp.einsum("ngop,ngpi->ngio", cr, m_re, precision=exact)
                - jnp.einsum("ngop,ngpi->ngio", ci, m_im, precision=exact))

    def compact(parts, row_axis, col_axis):
        t = jnp.stack([jnp.stack(row) for row in parts])
        t = t.reshape(2, 2, n_layers, nc, gc, t.shape[4], t.shape[5])
        t = t.transpose(2, 3, 0, 4, 5 + row_axis, 1, 5 + col_axis)
        return t.reshape(n_layers, nc, 2 * gc * t.shape[4], 2 * t.shape[6])

    def expand(x, group_rows, group_cols):
        c = x.shape[3] // 2
        tile = np.kron(np.eye(2), np.kron(np.ones((1, gc)), np.eye(c)))
        keep = group_rows[:, None] == group_cols[None, :]
        t = jnp.einsum("ncrj,jq->ncrq", x.astype(BF16), jnp.asarray(tile, BF16),
                       preferred_element_type=F32)
        return jnp.where(jnp.asarray(keep), t, 0.0).astype(BF16)

    group_of = lambda n, per_group: (np.arange(n) // per_group) % gc
    drive = expand(compact([[lb_re, lb_im], [bb_re, bb_im]], 1, 0),
                   group_of(2 * gc * gsz, gsz), group_of(2 * gc * n_state, n_state))
    cl_re, cl_im = c_times(ab_re, ab_im)
    cl2_re, cl2_im = c_times(a2_re, a2_im)
    k0 = re_c_times(bb_re, bb_im)
    k1 = re_c_times(lb_re, lb_im)
    read_rows = jnp.concatenate([compact([[cl_re, cl2_re], [-cl_im, -cl2_im]], 1, 0),
                                 compact([[k0, k1], [jnp.zeros_like(k0), k0]], 0, 1)], axis=2)
    read = expand(read_rows,
                  np.concatenate([group_of(2 * gc * n_state, n_state), group_of(2 * gc * gsz, gsz)]),
                  group_of(2 * gc * gsz, gsz))
    return dict(s5_drive=drive, s5_read=read, a2_re=a2_re.reshape(n_layers, nc, gc * n_state),
                a2_im=a2_im.reshape(n_layers, nc, gc * n_state))


def _softplus(z):
    return jnp.maximum(z, 0.0) + jnp.log1p(jnp.exp(-jnp.abs(z)))


def _rg_kernel(*refs, layer, ln_row, batch, steps, sub_steps, alpha, n_init):
    x_ref, refs = refs[0], refs[1:]
    init_refs, refs = refs[:n_init], refs[n_init:]
    (win_ref, cw_ref, cb_ref, wg_ref, bg_ref, lam_ref, wout_ref, g_ref, b_ref, out_ref, hn_ref) = refs[:11]
    hist_refs = refs[11:]
    width = hn_ref.shape[1]
    blk = width // RG_BLOCKS
    sub_rows = sub_steps * batch
    n_groups = batch // SUBLANES
    _init_carry((hn_ref, *hist_refs), init_refs)
    ln_g, ln_b = g_ref[ln_row[0], ln_row[1]:ln_row[1] + 1, :], b_ref[ln_row[0], ln_row[1]:ln_row[1] + 1, :]
    cw = [cw_ref[layer, k:k + 1, :] for k in range(cw_ref.shape[1])]
    cb = cb_ref[layer:layer + 1, :]
    b_r, b_i = bg_ref[layer:layer + 1, :width], bg_ref[layer:layer + 1, width:]

    c_softplus = RG_C * _softplus(-lam_ref[layer:layer + 1, :])
    h_state = [[hn_ref[rg * SUBLANES:(rg + 1) * SUBLANES, n * blk:(n + 1) * blk] for rg in range(n_groups)]
               for n in range(RG_BLOCKS)]

    def gate_block(xb, n):
        cols = slice(n * blk, (n + 1) * blk)
        zx = _dot(xb, win_ref[:, width + n * blk:width + (n + 1) * blk])
        xc = _causal_conv(zx, hist_refs, n * blk, [w[:, cols] for w in cw], cb[:, cols], batch)
        gts = _dot(xc.astype(BF16), wg_ref[n])
        r = jax.nn.sigmoid(gts[:, :blk] + b_r[:, cols])
        ig = jax.nn.sigmoid(gts[:, blk:] + b_i[:, cols])
        neg_log_a = r * c_softplus[:, cols]
        a = jnp.exp(-neg_log_a)
        th = jnp.tanh(neg_log_a)
        v = th / (1.0 + th)
        root = jnp.where(v > 0.0, v * lax.rsqrt(v), 0.0)
        bb = (math.sqrt(2.0) * root) * (ig * xc)

        def step(h, row):
            h = a[row:row + SUBLANES, :] * h + bb[row:row + SUBLANES, :]
            return h, (h,)

        h_state[n], (hs,) = _slab_scan(step, h_state[n], sub_steps, batch)
        gate = jax.nn.gelu(_dot(xb, win_ref[:, cols]))
        return (hs * gate).astype(BF16)

    def finish(s, x, hg):
        y = _dot(hg, wout_ref[...])
        out_ref[s * sub_rows:(s + 1) * sub_rows, :] = _layer_norm(alpha * x + y, ln_g, ln_b)

    pending = None
    for s in range(steps // sub_steps):
        x = x_ref[s * sub_rows:(s + 1) * sub_rows, :]
        xb = x.astype(BF16)
        blocks = []
        for n in range(RG_BLOCKS):
            blocks.append(gate_block(xb, n))
            if n == 0 and pending is not None:
                finish(*pending)
        pending = (s, x, jnp.concatenate(blocks, axis=1))
    finish(*pending)
    for n in range(RG_BLOCKS):
        for rg in range(n_groups):
            hn_ref[rg * SUBLANES:(rg + 1) * SUBLANES, n * blk:(n + 1) * blk] = h_state[n][rg]


def _rg_layer(h, init, p, j, ln_row, *, batch, seq_len, alpha):
    rows, d_model = h.shape
    width = p["rg_w_out"].shape[1]
    n_taps = p["rg_conv_w"].shape[1]
    steps, sub_steps = _tiling(seq_len, batch, n_taps - 1)
    operands = list(init) + [
        _pick(p["rg_w_in"], {0: j}), _whole(p["rg_conv_w"]), _whole(p["rg_conv_b"]),
        _pick(p["rg_w_gates"], {0: j}), _whole(p["rg_b_gates"]), _whole(p["rg_lam"]),
        _pick(p["rg_w_out"], {0: j}), _whole(p["ln_g"]), _whole(p["ln_b"])]
    kern = functools.partial(_rg_kernel, layer=j, ln_row=ln_row, batch=batch, steps=steps,
                             sub_steps=sub_steps, alpha=alpha, n_init=len(init))
    n_carry = n_taps
    return _call(
        kern, "rglru_mixer_ln", h, _row_specs(batch, steps, d_model, False), operands,
        [jax.ShapeDtypeStruct((rows, d_model), F32)]
        + [jax.ShapeDtypeStruct((batch, width), F32)] * n_carry,
        [_row_specs(batch, steps, d_model, False)] + [_carry_spec((batch, width))] * n_carry)


def _ffn_kernel(*refs, layer, ln_row, batch, steps, sub_steps, alpha, n_init, out_batch_major):
    x_ref, refs = refs[0], refs[1:]
    init_refs, refs = refs[:n_init], refs[n_init:]
    wup_ref, cw_ref, cb_ref, wd_ref, g_ref, b_ref, out_ref = refs[:7]
    hist_refs, act_scr = refs[7:-1], refs[-1]
    ff = wd_ref.shape[0]
    n_taps = cw_ref.shape[1]
    sub_rows = sub_steps * batch
    _init_carry(hist_refs, init_refs)
    ln_g, ln_b = g_ref[ln_row[0], ln_row[1]:ln_row[1] + 1, :], b_ref[ln_row[0], ln_row[1]:ln_row[1] + 1, :]

    def finish(s, x):
        y = _dot(act_scr[s % 2], wd_ref[...])
        _store_rows(out_ref, s * sub_steps, sub_steps, batch, out_batch_major,
                    _layer_norm(alpha * x + y, ln_g, ln_b))

    pending = None
    for s in range(steps // sub_steps):
        x = x_ref[s * sub_rows:(s + 1) * sub_rows, :]
        xb = x.astype(BF16)
        for j in range(ff // FF_TILE):
            val, gate = [
                _causal_conv(_dot(xb, wup_ref[:, col:col + FF_TILE]), hist_refs, col,
                             [cw_ref[layer, k:k + 1, col:col + FF_TILE] for k in range(n_taps)],
                             cb_ref[layer:layer + 1, col:col + FF_TILE], batch)
                for col in (j * FF_TILE, ff + j * FF_TILE)]
            act_scr[s % 2, :, j * FF_TILE:(j + 1) * FF_TILE] = (jax.nn.gelu(gate) * val).astype(BF16)
            if j == 0 and pending is not None:
                finish(*pending)
        pending = (s, x)
    finish(*pending)


def _ffn_layer(h, init, p, i, ln_row, *, batch, seq_len, alpha, out_batch_major):
    rows, d_model = h.shape
    ff = p["ffn_w_down"].shape[1]
    n_taps = p["ffn_conv_w"].shape[1]
    steps, sub_steps = _tiling(seq_len, batch, n_taps - 1)
    operands = list(init) + [
        _pick(p["ffn_w_up"], {0: i}), _whole(p["ffn_conv_w"]), _whole(p["ffn_conv_b"]),
        _pick(p["ffn_w_down"], {0: i}), _whole(p["ln_g"]), _whole(p["ln_b"])]
    kern = functools.partial(_ffn_kernel, layer=i, ln_row=ln_row, batch=batch, steps=steps,
                             sub_steps=sub_steps, alpha=alpha, n_init=len(init),
                             out_batch_major=out_batch_major)
    out_shape = (batch, seq_len, d_model) if out_batch_major else (rows, d_model)
    return _call(
        kern, "convffn_ln", h, _row_specs(batch, steps, d_model, False), operands,
        [jax.ShapeDtypeStruct(out_shape, F32)] + [jax.ShapeDtypeStruct((batch, 2 * ff), F32)] * (n_taps - 1),
        [_row_specs(batch, steps, d_model, out_batch_major)] + [_carry_spec((batch, 2 * ff))] * (n_taps - 1),
        scratch=[pltpu.VMEM((2, sub_steps * batch, ff), BF16)])


def _trunk(x, init, p, *, alpha):
    batch, seq_len, _ = x.shape
    depth = p["ffn_w_up"].shape[0]
    new = dict(s5_re=[], s5_im=[], rg_h=[], rg_conv=[], ffn_conv=[])
    kw = dict(batch=batch, seq_len=seq_len, alpha=alpha)
    h = x
    for i in range(depth):
        j = i // 2
        if i % 2 == 0:
            st = () if init is None else (init["s5_re"][j], init["s5_im"][j])
            h, sr, si = _s5_layer(h, st, p, j, (i, 0), **kw)
            new["s5_re"].append(sr)
            new["s5_im"].append(si)
        else:
            st = () if init is None else (init["rg_h"][j], *init["rg_conv"][j])
            h, hn, *hist = _rg_layer(h, st, p, j, (i, 0), **kw)
            new["rg_h"].append(hn)
            new["rg_conv"].append(hist)
        st = () if init is None else tuple(init["ffn_conv"][i])
        h, *hist = _ffn_layer(h, st, p, i, (i, 1), out_batch_major=(i == depth - 1), **kw)
        new["ffn_conv"].append(hist)
    return h, new


def _carried(new):
    one = lambda a: _whole(a)
    return dict(s5_re=[one(a) for a in new["s5_re"]], s5_im=[one(a) for a in new["s5_im"]],
                rg_h=[one(a) for a in new["rg_h"]],
                rg_conv=[[one(a) for a in hist] for hist in new["rg_conv"]],
                ffn_conv=[[one(a) for a in hist] for hist in new["ffn_conv"]])


def _given(s5_re, s5_im, rg_h, rg_conv, ffn_conv):
    flat = lambda a: [_whole(a[k].reshape(a.shape[1], -1)) for k in range(a.shape[0])]

    def steps(a):
        t = a.transpose(0, 2, 1, 3)
        return [[_pick(t, {0: n, 1: k}) for k in range(t.shape[1])] for n in range(t.shape[0])]

    return dict(s5_re=flat(s5_re), s5_im=flat(s5_im), rg_h=[_pick(rg_h, {0: n}) for n in range(rg_h.shape[0])],
                rg_conv=steps(rg_conv), ffn_conv=steps(ffn_conv))


def _stack_states(new, s5_shape):
    s5 = lambda k: jnp.stack(new[k]).reshape((len(new[k]), new[k][0].shape[0]) + s5_shape)
    conv = lambda k: jnp.stack([jnp.stack(hist, axis=1) for hist in new[k]])
    return (s5("s5_re"), s5("s5_im"), jnp.stack(new["rg_h"]), conv("rg_conv"), conv("ffn_conv"))


def kernel(x_prompt, x_sample, state_s5_re, state_s5_im, state_rg_h, state_rg_conv, state_ffn_conv,
           meta_tokens, s5_w_in, s5_lam_re, s5_lam_im, s5_log_step, s5_b_re, s5_b_im, s5_c_re,
           s5_c_im, s5_d, s5_w_out, rg_w_in, rg_conv_w, rg_conv_b, rg_w_gates, rg_b_gates, rg_lam,
           rg_w_out, ffn_w_up, ffn_conv_w, ffn_conv_b, ffn_w_down, ln_g, ln_b):
    depth = ffn_w_up.shape[0]
    alpha = (2 * depth) ** 0.25
    bp = x_prompt.shape[0]
    dt = x_prompt.dtype

    p = dict(
        s5_w_in=s5_w_in.astype(BF16), s5_w_out=s5_w_out.astype(BF16), s5_d=s5_d,
        rg_w_in=rg_w_in.astype(BF16), rg_conv_w=rg_conv_w, rg_conv_b=rg_conv_b,
        rg_w_gates=rg_w_gates.astype(BF16), rg_b_gates=rg_b_gates, rg_lam=rg_lam,
        rg_w_out=rg_w_out.astype(BF16),
        ffn_w_up=ffn_w_up.astype(BF16), ffn_conv_w=ffn_conv_w, ffn_conv_b=ffn_conv_b,
        ffn_w_down=ffn_w_down.astype(BF16), ln_g=ln_g, ln_b=ln_b,
        **_s5_params(s5_lam_re, s5_lam_im, s5_log_step, s5_b_re, s5_b_im, s5_c_re, s5_c_im))
    s5_shape = state_s5_re.shape[2:]

    x_meta = jnp.broadcast_to(meta_tokens[None].astype(dt), (bp,) + meta_tokens.shape)
    _, meta_states = _trunk(x_meta, None, p, alpha=alpha)
    y_prompt, new_p = _trunk(x_prompt, _carried(meta_states), p, alpha=alpha)
    y_sample, new_s = _trunk(x_sample, _given(state_s5_re, state_s5_im, state_rg_h, state_rg_conv,
                                              state_ffn_conv), p, alpha=alpha)
    return (y_prompt, y_sample) + _stack_states(new_p, s5_shape) + _stack_states(new_s, s5_shape)
```

```python
import functools
import math
from typing import NamedTuple

import jax
import jax.numpy as jnp
import numpy as np
from jax import lax
from jax.experimental import pallas as pl
from jax.experimental.pallas import tpu as pltpu

RG_BLOCKS = 4
RG_C = 8.0
LN_EPS = 1e-5

SUBLANES = 8
S5_CHUNK_GROUPS = 8
MXU_COLS = 256
FF_TILE = MXU_COLS
BLOCK_ROWS = 1024
SUB_ROWS = 512
VMEM_LIMIT_BYTES = 56 * 1024 * 1024

BF16 = jnp.bfloat16
F32 = jnp.float32


class _Seg(NamedTuple):
    batch: int
    seq_len: int
    steps: int
    sub_steps: int
    n_init: int
    x_batch_major: bool
    out_batch_major: bool


def _dot(a, b):
    return jnp.dot(a, b, preferred_element_type=F32)


def _layer_norm(z, g, b):
    mu = jnp.mean(z, axis=-1, keepdims=True)
    zc = z - mu
    var = jnp.mean(zc * zc, axis=-1, keepdims=True)
    return zc * lax.rsqrt(var + LN_EPS) * g + b


def _largest_divisor(n, limit):
    return max(d for d in range(1, n + 1) if n % d == 0 and d <= limit)


def _segments(hs, dims, inits, min_steps, out_batch_major):
    segs = []
    for h, (batch, seq_len), init in zip(hs, dims, inits):
        steps = _largest_divisor(seq_len, max(BLOCK_ROWS // batch, min_steps))
        sub = _largest_divisor(steps, max(SUB_ROWS // batch, min_steps))
        assert sub >= min_steps and (len(hs) == 1 or steps == seq_len), (seq_len, batch, min_steps)
        segs.append(_Seg(batch, seq_len, steps, sub, len(init), h.ndim == 3, out_batch_major))
    return tuple(segs)


def _whole(a):
    return a, pl.BlockSpec(a.shape, lambda i: (0,) * a.ndim, pipeline_mode=pl.Buffered(1))


def _pick(a, fixed):
    block = tuple(None if d in fixed else n for d, n in enumerate(a.shape))
    index = tuple(fixed.get(d, 0) for d in range(a.ndim))
    return a, pl.BlockSpec(block, lambda i: index, pipeline_mode=pl.Buffered(1))


def _row_spec(sg, d_model, batch_major):
    if batch_major:
        return pl.BlockSpec((sg.batch, sg.steps, d_model), lambda i: (0, i, 0))
    return pl.BlockSpec((sg.steps * sg.batch, d_model), lambda i: (i, 0))


def _row_shape(sg, d_model, batch_major):
    shape = (sg.batch, sg.seq_len, d_model) if batch_major else (sg.seq_len * sg.batch, d_model)
    return jax.ShapeDtypeStruct(shape, F32)


def _carry_spec(shape):
    return pl.BlockSpec(shape, lambda i: (0,) * len(shape))


def _load_rows(x_ref, t0, n_steps, batch, batch_major, order=None):
    if batch_major:
        return jnp.concatenate([x_ref[:, t0 + t, :] for t in (order or range(n_steps))], axis=0)
    if order is None:
        return x_ref[t0 * batch:(t0 + n_steps) * batch, :]
    return jnp.concatenate([x_ref[(t0 + t) * batch:(t0 + t + 1) * batch, :] for t in order], axis=0)


def _store_rows(out_ref, t0, n_steps, batch, batch_major, val, order=None):
    if batch_major or order is not None:
        for i, t in enumerate(order or range(n_steps)):
            if batch_major:
                out_ref[:, t0 + t, :] = val[i * batch:(i + 1) * batch, :]
            else:
                out_ref[(t0 + t) * batch:(t0 + t + 1) * batch, :] = val[i * batch:(i + 1) * batch, :]
    else:
        out_ref[t0 * batch:(t0 + n_steps) * batch, :] = val


def _init_carry(carry_refs, init_refs):
    @pl.when(pl.program_id(0) == 0)
    def _():
        for k, ref in enumerate(carry_refs):
            ref[...] = init_refs[k][...] if init_refs else jnp.zeros_like(ref)


def _slab_scan(step_fn, carry, n_steps, batch):
    n_groups = batch // SUBLANES
    outs = [[None] * n_groups for _ in range(n_steps)]
    carry = list(carry)
    for t in range(n_steps):
        for rg in range(n_groups):
            row = t * batch + rg * SUBLANES
            carry[rg], outs[t][rg] = step_fn(carry[rg], row)
    n_out = len(outs[0][0])
    stacked = [jnp.concatenate([outs[t][rg][k] for t in range(n_steps) for rg in range(n_groups)], axis=0)
               for k in range(n_out)]
    return carry, stacked


def _causal_conv(up, hist_refs, col, cw, cb, batch):
    rows, width = up.shape
    n_hist = len(hist_refs)
    ext = jnp.concatenate([r[:, col:col + width] for r in hist_refs] + [up], axis=0)
    for k in range(n_hist):
        hist_refs[k][:, col:col + width] = ext[rows + k * batch:rows + (k + 1) * batch]
    y = cb + cw[0] * ext[0:rows]
    for k in range(1, n_hist + 1):
        y = y + cw[k] * ext[k * batch:k * batch + rows]
    return y


def _split_refs(refs, segs, n_weights, n_outs):
    xs, pos = refs[:len(segs)], len(segs)
    inits = []
    for sg in segs:
        inits.append(refs[pos:pos + sg.n_init])
        pos += sg.n_init
    weights, pos = refs[pos:pos + n_weights], pos + n_weights
    outs = []
    for _ in segs:
        outs.append(refs[pos:pos + n_outs])
        pos += n_outs
    return xs, inits, weights, outs, refs[pos:]


def _call(kern, name, segs, hs, inits, weights, out_shapes, out_specs, scratch=()):
    d_model = hs[0].shape[-1]
    operands = [pair for init in inits for pair in init] + list(weights)
    arrays, specs = zip(*operands)
    n_blocks = {sg.seq_len // sg.steps for sg in segs}
    assert len(n_blocks) == 1, segs
    return pl.pallas_call(
        kern,
        grid=(n_blocks.pop(),),
        in_specs=[_row_spec(sg, d_model, sg.x_batch_major) for sg in segs] + list(specs),
        out_specs=[spec for per_seg in out_specs for spec in per_seg],
        out_shape=[shape for per_seg in out_shapes for shape in per_seg],
        scratch_shapes=list(scratch),
        compiler_params=pltpu.CompilerParams(
            dimension_semantics=("arbitrary",), vmem_limit_bytes=VMEM_LIMIT_BYTES),
        name=name,
    )(*hs, *arrays)


def _per_segment(flat, segs, n_outs):
    return [flat[k * n_outs:(k + 1) * n_outs] for k in range(len(segs))]


S5_WEIGHTS = 9


def _s5_kernel(*refs, layer, ln_row, alpha, segs):
    xs, inits, weights, outs, _ = _split_refs(refs, segs, S5_WEIGHTS, 3)
    for x_ref, init_refs, out_refs, sg in zip(xs, inits, outs, segs):
        _s5_segment(x_ref, init_refs, weights, out_refs, sg, layer, ln_row, alpha)


def _s5_segment(x_ref, init_refs, weights, out_refs, sg, layer, ln_row, alpha):
    win_ref, drive_ref, read_ref, a2re_ref, a2im_ref, d_ref, wout_ref, g_ref, b_ref = weights
    out_ref, sre_ref, sim_ref = out_refs
    batch, sub_steps = sg.batch, sg.sub_steps
    d_model = out_ref.shape[1]
    n_chunks, cin2, cst2 = drive_ref.shape
    cin = cin2 // 2
    cst = cst2 // 2
    n_pairs = sub_steps // 2
    pair_rows = n_pairs * batch
    n_groups = batch // SUBLANES
    order = list(range(0, sub_steps, 2)) + list(range(1, sub_steps, 2))
    _init_carry((sre_ref, sim_ref), init_refs)
    ln_g, ln_b = g_ref[ln_row[0], ln_row[1]:ln_row[1] + 1, :], b_ref[ln_row[0], ln_row[1]:ln_row[1] + 1, :]
    d_skip = d_ref[layer:layer + 1, :]

    def rows_of(ref, rg, c):
        return ref[rg * SUBLANES:(rg + 1) * SUBLANES, c * cst:(c + 1) * cst]

    state = [[(rows_of(sre_ref, rg, c), rows_of(sim_ref, rg, c)) for rg in range(n_groups)]
             for c in range(n_chunks)]

    def pair_inputs(u, c):
        cols = slice(c * cin, (c + 1) * cin)
        return jnp.concatenate([u[:pair_rows, cols], u[pair_rows:, cols]], axis=1).astype(BF16)

    def scan_readout(up, drv, c):
        a_re = jnp.broadcast_to(a2re_ref[c:c + 1, :], (SUBLANES, cst))
        a_im = jnp.broadcast_to(a2im_ref[c:c + 1, :], (SUBLANES, cst))

        def step(carry, row):
            s_re, s_im = carry
            n_re = a_re * s_re - a_im * s_im + drv[row:row + SUBLANES, 0:cst]
            n_im = a_re * s_im + a_im * s_re + drv[row:row + SUBLANES, cst:cst2]
            return (n_re, n_im), (s_re, s_im)

        state[c], (st_re, st_im) = _slab_scan(step, state[c], n_pairs, batch)
        lhs = jnp.concatenate([st_re.astype(BF16), st_im.astype(BF16), up], axis=1)
        return _dot(lhs, read_ref[c])

    def finish(s, x, u, ys):
        y = jnp.concatenate([jnp.concatenate([yp[:, :cin] for yp in ys], axis=1),
                             jnp.concatenate([yp[:, cin:] for yp in ys], axis=1)], axis=0) + d_skip * u
        vg = _dot(jax.nn.gelu(y).astype(BF16), wout_ref[...])
        mix = vg[:, :d_model] * jax.nn.sigmoid(vg[:, d_model:])
        _store_rows(out_ref, s * sub_steps, sub_steps, batch, False,
                    _layer_norm(alpha * x + mix, ln_g, ln_b), order)

    pending = None
    for s in range(sg.steps // sub_steps):
        x = _load_rows(x_ref, s * sub_steps, sub_steps, batch, sg.x_batch_major, order)
        u = _dot(x.astype(BF16), win_ref[...])
        ys = []
        up_next = pair_inputs(u, 0)
        drv_next = _dot(up_next, drive_ref[0])
        for c in range(n_chunks):
            up, drv = up_next, drv_next
            if c + 1 < n_chunks:
                up_next = pair_inputs(u, c + 1)
                drv_next = _dot(up_next, drive_ref[c + 1])
            ys.append(scan_readout(up, drv, c))
            if c == 0 and pending is not None:
                finish(*pending)
        pending = (s, x, u, ys)
    finish(*pending)
    for c in range(n_chunks):
        for rg in range(n_groups):
            sre_ref[rg * SUBLANES:(rg + 1) * SUBLANES, c * cst:(c + 1) * cst] = state[c][rg][0]
            sim_ref[rg * SUBLANES:(rg + 1) * SUBLANES, c * cst:(c + 1) * cst] = state[c][rg][1]


def _s5_layer(hs, dims, inits, p, j, ln_row, alpha):
    d_model = hs[0].shape[-1]
    segs = _segments(hs, dims, inits, 2, False)
    assert all(sg.sub_steps % 2 == 0 for sg in segs), segs
    n_state = p["a2_re"].shape[1] * p["a2_re"].shape[2]
    weights = [
        _pick(p["s5_w_in"], {0: j}), _pick(p["s5_drive"], {0: j}), _pick(p["s5_read"], {0: j}),
        _pick(p["a2_re"], {0: j}), _pick(p["a2_im"], {0: j}), _whole(p["s5_d"]),
        _pick(p["s5_w_out"], {0: j}), _whole(p["ln_g"]), _whole(p["ln_b"])]
    assert len(weights) == S5_WEIGHTS
    kern = functools.partial(_s5_kernel, layer=j, ln_row=ln_row, alpha=alpha, segs=segs)
    state = lambda sg: jax.ShapeDtypeStruct((sg.batch, n_state), F32)
    flat = _call(
        kern, "s5_mixer_ln", segs, hs, inits, weights,
        [[_row_shape(sg, d_model, False), state(sg), state(sg)] for sg in segs],
        [[_row_spec(sg, d_model, False)] + [_carry_spec((sg.batch, n_state))] * 2 for sg in segs])
    return _per_segment(flat, segs, 3)


def _s5_params(lam_re, lam_im, log_step, b_re, b_im, c_re, c_im):
    n_layers, n_groups, n_state, gsz = b_re.shape
    gc = S5_CHUNK_GROUPS
    nc = n_groups // gc
    exact = lax.Precision.HIGHEST
    step = jnp.exp(log_step.astype(F32))[..., None]
    lr, li = lam_re.astype(F32), lam_im.astype(F32)
    mag = jnp.exp(lr * step)
    ab_re, ab_im = mag * jnp.cos(li * step), mag * jnp.sin(li * step)
    den = lr * lr + li * li
    nr, ni = ab_re - 1.0, ab_im
    q_re = ((nr * lr + ni * li) / den)[..., None]
    q_im = ((ni * lr - nr * li) / den)[..., None]
    br, bi = b_re.astype(F32), b_im.astype(F32)
    bb_re = q_re * br - q_im * bi
    bb_im = q_re * bi + q_im * br
    a2_re, a2_im = ab_re * ab_re - ab_im * ab_im, 2.0 * ab_re * ab_im
    lb_re = ab_re[..., None] * bb_re - ab_im[..., None] * bb_im
    lb_im = ab_re[..., None] * bb_im + ab_im[..., None] * bb_re
    cr, ci = c_re.astype(F32), c_im.astype(F32)

    def c_times(m_re, m_im):
        m_re, m_im = m_re[:, :, None, :], m_im[:, :, None, :]
        return cr * m_re - ci * m_im, cr * m_im + ci * m_re

    def re_c_times(m_re, m_im):
        return (jnp.einsum("ngop,ngpi->ngio", cr, m_re, precision=exact)
                - jnp.einsum("ngop,ngpi->ngio", ci, m_im, precision=exact))

    def compact(parts, row_axis, col_axis):
        t = jnp.stack([jnp.stack(row) for row in parts])
        t = t.reshape(2, 2, n_layers, nc, gc, t.shape[4], t.shape[5])
        t = t.transpose(2, 3, 0, 4, 5 + row_axis, 1, 5 + col_axis)
        return t.reshape(n_layers, nc, 2 * gc * t.shape[4], 2 * t.shape[6])

    def expand(x, group_rows, group_cols):
        c = x.shape[3] // 2
        tile = np.kron(np.eye(2), np.kron(np.ones((1, gc)), np.eye(c)))
        keep = group_rows[:, None] == group_cols[None, :]
        t = jnp.einsum("ncrj,jq->ncrq", x.astype(BF16), jnp.asarray(tile, BF16),
                       preferred_element_type=F32)
        return jnp.where(jnp.asarray(keep), t, 0.0).astype(BF16)

    group_of = lambda n, per_group: (np.arange(n) // per_group) % gc
    drive = expand(compact([[lb_re, lb_im], [bb_re, bb_im]], 1, 0),
                   group_of(2 * gc * gsz, gsz), group_of(2 * gc * n_state, n_state))
    cl_re, cl_im = c_times(ab_re, ab_im)
    cl2_re, cl2_im = c_times(a2_re, a2_im)
    k0 = re_c_times(bb_re, bb_im)
    k1 = re_c_times(lb_re, lb_im)
    read_rows = jnp.concatenate([compact([[cl_re, cl2_re], [-cl_im, -cl2_im]], 1, 0),
                                 compact([[k0, k1], [jnp.zeros_like(k0), k0]], 0, 1)], axis=2)
    read = expand(read_rows,
                  np.concatenate([group_of(2 * gc * n_state, n_state), group_of(2 * gc * gsz, gsz)]),
                  group_of(2 * gc * gsz, gsz))
    return dict(s5_drive=drive, s5_read=read, a2_re=a2_re.reshape(n_layers, nc, gc * n_state),
                a2_im=a2_im.reshape(n_layers, nc, gc * n_state))


RG_WEIGHTS = 9


def _softplus(z):
    return jnp.maximum(z, 0.0) + jnp.log1p(jnp.exp(-jnp.abs(z)))


def _rg_kernel(*refs, layer, ln_row, alpha, segs, n_carry):
    xs, inits, weights, outs, _ = _split_refs(refs, segs, RG_WEIGHTS, 1 + n_carry)
    for x_ref, init_refs, out_refs, sg in zip(xs, inits, outs, segs):
        _rg_segment(x_ref, init_refs, weights, out_refs, sg, layer, ln_row, alpha)


def _rg_segment(x_ref, init_refs, weights, out_refs, sg, layer, ln_row, alpha):
    win_ref, cw_ref, cb_ref, wg_ref, bg_ref, lam_ref, wout_ref, g_ref, b_ref = weights
    out_ref, hn_ref, hist_refs = out_refs[0], out_refs[1], out_refs[2:]
    batch, sub_steps = sg.batch, sg.sub_steps
    width = hn_ref.shape[1]
    blk = width // RG_BLOCKS
    sub_rows = sub_steps * batch
    n_groups = batch // SUBLANES
    _init_carry((hn_ref, *hist_refs), init_refs)
    ln_g, ln_b = g_ref[ln_row[0], ln_row[1]:ln_row[1] + 1, :], b_ref[ln_row[0], ln_row[1]:ln_row[1] + 1, :]
    cw = [cw_ref[layer, k:k + 1, :] for k in range(cw_ref.shape[1])]
    cb = cb_ref[layer:layer + 1, :]
    b_r, b_i = bg_ref[layer:layer + 1, :width], bg_ref[layer:layer + 1, width:]

    c_softplus = RG_C * _softplus(-lam_ref[layer:layer + 1, :])
    h_state = [[hn_ref[rg * SUBLANES:(rg + 1) * SUBLANES, n * blk:(n + 1) * blk] for rg in range(n_groups)]
               for n in range(RG_BLOCKS)]

    def gate_block(xb, n):
        cols = slice(n * blk, (n + 1) * blk)
        zx = _dot(xb, win_ref[:, width + n * blk:width + (n + 1) * blk])
        xc = _causal_conv(zx, hist_refs, n * blk, [w[:, cols] for w in cw], cb[:, cols], batch)
        gts = _dot(xc.astype(BF16), wg_ref[n])
        r = jax.nn.sigmoid(gts[:, :blk] + b_r[:, cols])
        ig = jax.nn.sigmoid(gts[:, blk:] + b_i[:, cols])
        neg_log_a = r * c_softplus[:, cols]
        a = jnp.exp(-neg_log_a)
        th = jnp.tanh(neg_log_a)
        v = th / (1.0 + th)
        root = jnp.where(v > 0.0, v * lax.rsqrt(v), 0.0)
        bb = (math.sqrt(2.0) * root) * (ig * xc)

        def step(h, row):
            h = a[row:row + SUBLANES, :] * h + bb[row:row + SUBLANES, :]
            return h, (h,)

        h_state[n], (hs,) = _slab_scan(step, h_state[n], sub_steps, batch)
        gate = jax.nn.gelu(_dot(xb, win_ref[:, cols]))
        return (hs * gate).astype(BF16)

    def finish(s, x, hg):
        y = _dot(hg, wout_ref[...])
        out_ref[s * sub_rows:(s + 1) * sub_rows, :] = _layer_norm(alpha * x + y, ln_g, ln_b)

    pending = None
    for s in range(sg.steps // sub_steps):
        x = x_ref[s * sub_rows:(s + 1) * sub_rows, :]
        xb = x.astype(BF16)
        blocks = []
        for n in range(RG_BLOCKS):
            blocks.append(gate_block(xb, n))
            if n == 0 and pending is not None:
                finish(*pending)
        pending = (s, x, jnp.concatenate(blocks, axis=1))
    finish(*pending)
    for n in range(RG_BLOCKS):
        for rg in range(n_groups):
            hn_ref[rg * SUBLANES:(rg + 1) * SUBLANES, n * blk:(n + 1) * blk] = h_state[n][rg]


def _rg_layer(hs, dims, inits, p, j, ln_row, alpha):
    d_model = hs[0].shape[-1]
    width = p["rg_w_out"].shape[1]
    n_taps = p["rg_conv_w"].shape[1]
    segs = _segments(hs, dims, inits, n_taps - 1, False)
    weights = [
        _pick(p["rg_w_in"], {0: j}), _whole(p["rg_conv_w"]), _whole(p["rg_conv_b"]),
        _pick(p["rg_w_gates"], {0: j}), _whole(p["rg_b_gates"]), _whole(p["rg_lam"]),
        _pick(p["rg_w_out"], {0: j}), _whole(p["ln_g"]), _whole(p["ln_b"])]
    assert len(weights) == RG_WEIGHTS
    n_carry = n_taps
    kern = functools.partial(_rg_kernel, layer=j, ln_row=ln_row, alpha=alpha, segs=segs, n_carry=n_carry)
    flat = _call(
        kern, "rglru_mixer_ln", segs, hs, inits, weights,
        [[_row_shape(sg, d_model, False)] + [jax.ShapeDtypeStruct((sg.batch, width), F32)] * n_carry
         for sg in segs],
        [[_row_spec(sg, d_model, False)] + [_carry_spec((sg.batch, width))] * n_carry for sg in segs])
    return _per_segment(flat, segs, 1 + n_carry)


FFN_WEIGHTS = 6


def _ffn_kernel(*refs, layer, ln_row, alpha, segs, n_hist):
    xs, inits, weights, outs, scratch = _split_refs(refs, segs, FFN_WEIGHTS, 1 + n_hist)
    for x_ref, init_refs, out_refs, act_scr, sg in zip(xs, inits, outs, scratch, segs):
        _ffn_segment(x_ref, init_refs, weights, out_refs, act_scr, sg, layer, ln_row, alpha)


def _ffn_segment(x_ref, init_refs, weights, out_refs, act_scr, sg, layer, ln_row, alpha):
    wup_ref, cw_ref, cb_ref, wd_ref, g_ref, b_ref = weights
    out_ref, hist_refs = out_refs[0], out_refs[1:]
    batch, sub_steps = sg.batch, sg.sub_steps
    ff = wd_ref.shape[0]
    n_taps = cw_ref.shape[1]
    sub_rows = sub_steps * batch
    _init_carry(hist_refs, init_refs)
    ln_g, ln_b = g_ref[ln_row[0], ln_row[1]:ln_row[1] + 1, :], b_ref[ln_row[0], ln_row[1]:ln_row[1] + 1, :]

    def finish(s, x):
        y = _dot(act_scr[s % 2], wd_ref[...])
        _store_rows(out_ref, s * sub_steps, sub_steps, batch, sg.out_batch_major,
                    _layer_norm(alpha * x + y, ln_g, ln_b))

    pending = None
    for s in range(sg.steps // sub_steps):
        x = x_ref[s * sub_rows:(s + 1) * sub_rows, :]
        xb = x.astype(BF16)
        for j in range(ff // FF_TILE):
            val, gate = [
                _causal_conv(_dot(xb, wup_ref[:, col:col + FF_TILE]), hist_refs, col,
                             [cw_ref[layer, k:k + 1, col:col + FF_TILE] for k in range(n_taps)],
                             cb_ref[layer:layer + 1, col:col + FF_TILE], batch)
                for col in (j * FF_TILE, ff + j * FF_TILE)]
            act_scr[s % 2, :, j * FF_TILE:(j + 1) * FF_TILE] = (jax.nn.gelu(gate) * val).astype(BF16)
            if j == 0 and pending is not None:
                finish(*pending)
        pending = (s, x)
    finish(*pending)


def _ffn_layer(hs, dims, inits, p, i, ln_row, alpha, out_batch_major):
    d_model = hs[0].shape[-1]
    ff = p["ffn_w_down"].shape[1]
    n_taps = p["ffn_conv_w"].shape[1]
    segs = _segments(hs, dims, inits, n_taps - 1, out_batch_major)
    weights = [
        _pick(p["ffn_w_up"], {0: i}), _whole(p["ffn_conv_w"]), _whole(p["ffn_conv_b"]),
        _pick(p["ffn_w_down"], {0: i}), _whole(p["ln_g"]), _whole(p["ln_b"])]
    assert len(weights) == FFN_WEIGHTS
    kern = functools.partial(_ffn_kernel, layer=i, ln_row=ln_row, alpha=alpha, segs=segs, n_hist=n_taps - 1)
    flat = _call(
        kern, "convffn_ln", segs, hs, inits, weights,
        [[_row_shape(sg, d_model, out_batch_major)]
         + [jax.ShapeDtypeStruct((sg.batch, 2 * ff), F32)] * (n_taps - 1) for sg in segs],
        [[_row_spec(sg, d_model, out_batch_major)] + [_carry_spec((sg.batch, 2 * ff))] * (n_taps - 1)
         for sg in segs],
        scratch=[pltpu.VMEM((2, sg.sub_steps * sg.batch, ff), BF16) for sg in segs])
    return _per_segment(flat, segs, n_taps)


def _trunks(xs, inits, p, alpha):
    dims = [x.shape[:2] for x in xs]
    depth = p["ffn_w_up"].shape[0]
    news = [dict(s5_re=[], s5_im=[], rg_h=[], rg_conv=[], ffn_conv=[]) for _ in xs]
    hs = list(xs)
    for i in range(depth):
        j = i // 2
        if i % 2 == 0:
            st = [() if init is None else (init["s5_re"][j], init["s5_im"][j]) for init in inits]
            outs = _s5_layer(hs, dims, st, p, j, (i, 0), alpha)
            for new, (_, sr, si) in zip(news, outs):
                new["s5_re"].append(sr)
                new["s5_im"].append(si)
        else:
            st = [() if init is None else (init["rg_h"][j], *init["rg_conv"][j]) for init in inits]
            outs = _rg_layer(hs, dims, st, p, j, (i, 0), alpha)
            for new, (_, hn, *hist) in zip(news, outs):
                new["rg_h"].append(hn)
                new["rg_conv"].append(hist)
        hs = [o[0] for o in outs]
        st = [() if init is None else tuple(init["ffn_conv"][i]) for init in inits]
        outs = _ffn_layer(hs, dims, st, p, i, (i, 1), alpha, out_batch_major=(i == depth - 1))
        for new, (_, *hist) in zip(news, outs):
            new["ffn_conv"].append(hist)
        hs = [o[0] for o in outs]
    return hs, news


def _carried(new):
    one = lambda a: _whole(a)
    return dict(s5_re=[one(a) for a in new["s5_re"]], s5_im=[one(a) for a in new["s5_im"]],
                rg_h=[one(a) for a in new["rg_h"]],
                rg_conv=[[one(a) for a in hist] for hist in new["rg_conv"]],
                ffn_conv=[[one(a) for a in hist] for hist in new["ffn_conv"]])


def _given(s5_re, s5_im, rg_h, rg_conv, ffn_conv):
    flat = lambda a: [_whole(a[k].reshape(a.shape[1], -1)) for k in range(a.shape[0])]

    def steps(a):
        t = a.transpose(0, 2, 1, 3)
        return [[_pick(t, {0: n, 1: k}) for k in range(t.shape[1])] for n in range(t.shape[0])]

    return dict(s5_re=flat(s5_re), s5_im=flat(s5_im), rg_h=[_pick(rg_h, {0: n}) for n in range(rg_h.shape[0])],
                rg_conv=steps(rg_conv), ffn_conv=steps(ffn_conv))


def _stack_states(new, s5_shape):
    s5 = lambda k: jnp.stack(new[k]).reshape((len(new[k]), new[k][0].shape[0]) + s5_shape)
    conv = lambda k: jnp.stack([jnp.stack(hist, axis=1) for hist in new[k]])
    return (s5("s5_re"), s5("s5_im"), jnp.stack(new["rg_h"]), conv("rg_conv"), conv("ffn_conv"))


def kernel(x_prompt, x_sample, state_s5_re, state_s5_im, state_rg_h, state_rg_conv, state_ffn_conv,
           meta_tokens, s5_w_in, s5_lam_re, s5_lam_im, s5_log_step, s5_b_re, s5_b_im, s5_c_re,
           s5_c_im, s5_d, s5_w_out, rg_w_in, rg_conv_w, rg_conv_b, rg_w_gates, rg_b_gates, rg_lam,
           rg_w_out, ffn_w_up, ffn_conv_w, ffn_conv_b, ffn_w_down, ln_g, ln_b):
    depth = ffn_w_up.shape[0]
    alpha = (2 * depth) ** 0.25
    bp = x_prompt.shape[0]
    dt = x_prompt.dtype

    p = dict(
        s5_w_in=s5_w_in.astype(BF16), s5_w_out=s5_w_out.astype(BF16), s5_d=s5_d,
        rg_w_in=rg_w_in.astype(BF16), rg_conv_w=rg_conv_w, rg_conv_b=rg_conv_b,
        rg_w_gates=rg_w_gates.astype(BF16), rg_b_gates=rg_b_gates, rg_lam=rg_lam,
        rg_w_out=rg_w_out.astype(BF16),
        ffn_w_up=ffn_w_up.astype(BF16), ffn_conv_w=ffn_conv_w, ffn_conv_b=ffn_conv_b,
        ffn_w_down=ffn_w_down.astype(BF16), ln_g=ln_g, ln_b=ln_b,
        **_s5_params(s5_lam_re, s5_lam_im, s5_log_step, s5_b_re, s5_b_im, s5_c_re, s5_c_im))
    s5_shape = state_s5_re.shape[2:]

    x_meta = jnp.broadcast_to(meta_tokens[None].astype(dt), (bp,) + meta_tokens.shape)
    given = _given(state_s5_re, state_s5_im, state_rg_h, state_rg_conv, state_ffn_conv)
    (_, y_sample), (meta_states, new_s) = _trunks([x_meta, x_sample], [None, given], p, alpha)
    (y_prompt,), (new_p,) = _trunks([x_prompt], [_carried(meta_states)], p, alpha)
    return (y_prompt, y_sample) + _stack_states(new_p, s5_shape) + _stack_states(new_s, s5_shape)
```

```python
import functools
import math
from typing import NamedTuple

import jax
import jax.numpy as jnp
import numpy as np
from jax import lax
from jax.experimental import pallas as pl
from jax.experimental.pallas import tpu as pltpu

RG_BLOCKS = 4
RG_C = 8.0
LN_EPS = 1e-5

SUBLANES = 8
S5_CHUNK_GROUPS = 8
MXU_COLS = 256
FF_TILE = MXU_COLS
BLOCK_ROWS = 1024
SUB_ROWS = 512
VMEM_LIMIT_BYTES = 56 * 1024 * 1024

BF16 = jnp.bfloat16
F32 = jnp.float32


class _Seg(NamedTuple):
    batch: int
    seq_len: int
    steps: int
    sub_steps: int
    n_init: int
    x_batch_major: bool
    out_batch_major: bool


def _dot(a, b):
    return jnp.dot(a, b, preferred_element_type=F32)


def _layer_norm(z, g, b):
    mu = jnp.mean(z, axis=-1, keepdims=True)
    zc = z - mu
    var = jnp.mean(zc * zc, axis=-1, keepdims=True)
    return zc * lax.rsqrt(var + LN_EPS) * g + b


def _largest_divisor(n, limit):
    return max(d for d in range(1, n + 1) if n % d == 0 and d <= limit)


def _segments(hs, dims, inits, min_steps, out_batch_major):
    segs = []
    for h, (batch, seq_len), init in zip(hs, dims, inits):
        steps = _largest_divisor(seq_len, max(BLOCK_ROWS // batch, min_steps))
        sub = _largest_divisor(steps, max(SUB_ROWS // batch, min_steps))
        assert sub >= min_steps and (len(hs) == 1 or steps == seq_len), (seq_len, batch, min_steps)
        segs.append(_Seg(batch, seq_len, steps, sub, len(init), h.ndim == 3, out_batch_major))
    return tuple(segs)


def _whole(a):
    return a, pl.BlockSpec(a.shape, lambda i: (0,) * a.ndim, pipeline_mode=pl.Buffered(1))


def _pick(a, fixed):
    block = tuple(None if d in fixed else n for d, n in enumerate(a.shape))
    index = tuple(fixed.get(d, 0) for d in range(a.ndim))
    return a, pl.BlockSpec(block, lambda i: index, pipeline_mode=pl.Buffered(1))


def _row_spec(sg, d_model, batch_major):
    if batch_major:
        return pl.BlockSpec((sg.batch, sg.steps, d_model), lambda i: (0, i, 0))
    return pl.BlockSpec((sg.steps * sg.batch, d_model), lambda i: (i, 0))


def _row_shape(sg, d_model, batch_major):
    shape = (sg.batch, sg.seq_len, d_model) if batch_major else (sg.seq_len * sg.batch, d_model)
    return jax.ShapeDtypeStruct(shape, F32)


def _carry_spec(shape):
    return pl.BlockSpec(shape, lambda i: (0,) * len(shape))


def _load_rows(x_ref, t0, n_steps, batch, batch_major, order=None):
    if batch_major:
        return jnp.concatenate([x_ref[:, t0 + t, :] for t in (order or range(n_steps))], axis=0)
    if order is None:
        return x_ref[t0 * batch:(t0 + n_steps) * batch, :]
    return jnp.concatenate([x_ref[(t0 + t) * batch:(t0 + t + 1) * batch, :] for t in order], axis=0)


def _store_rows(out_ref, t0, n_steps, batch, batch_major, val, order=None):
    if batch_major or order is not None:
        for i, t in enumerate(order or range(n_steps)):
            if batch_major:
                out_ref[:, t0 + t, :] = val[i * batch:(i + 1) * batch, :]
            else:
                out_ref[(t0 + t) * batch:(t0 + t + 1) * batch, :] = val[i * batch:(i + 1) * batch, :]
    else:
        out_ref[t0 * batch:(t0 + n_steps) * batch, :] = val


def _init_carry(carry_refs, init_refs):
    @pl.when(pl.program_id(0) == 0)
    def _():
        for k, ref in enumerate(carry_refs):
            ref[...] = init_refs[k][...] if init_refs else jnp.zeros_like(ref)


def _slab_scan(step_fn, carry, n_steps, batch):
    n_groups = batch // SUBLANES
    outs = [[None] * n_groups for _ in range(n_steps)]
    carry = list(carry)
    for t in range(n_steps):
        for rg in range(n_groups):
            row = t * batch + rg * SUBLANES
            carry[rg], outs[t][rg] = step_fn(carry[rg], row)
    n_out = len(outs[0][0])
    stacked = [jnp.concatenate([outs[t][rg][k] for t in range(n_steps) for rg in range(n_groups)], axis=0)
               for k in range(n_out)]
    return carry, stacked


def _causal_conv(up, hist_refs, col, cw, cb, batch):
    rows, width = up.shape
    n_hist = len(hist_refs)
    ext = jnp.concatenate([r[:, col:col + width] for r in hist_refs] + [up], axis=0)
    for k in range(n_hist):
        hist_refs[k][:, col:col + width] = ext[rows + k * batch:rows + (k + 1) * batch]
    y = cb + cw[0] * ext[0:rows]
    for k in range(1, n_hist + 1):
        y = y + cw[k] * ext[k * batch:k * batch + rows]
    return y


def _split_refs(refs, segs, n_weights, n_outs):
    xs, pos = refs[:len(segs)], len(segs)
    inits = []
    for sg in segs:
        inits.append(refs[pos:pos + sg.n_init])
        pos += sg.n_init
    weights, pos = refs[pos:pos + n_weights], pos + n_weights
    outs = []
    for _ in segs:
        outs.append(refs[pos:pos + n_outs])
        pos += n_outs
    return xs, inits, weights, outs, refs[pos:]


def _call(kern, name, segs, hs, inits, weights, out_shapes, out_specs, scratch=()):
    d_model = hs[0].shape[-1]
    operands = [pair for init in inits for pair in init] + list(weights)
    arrays, specs = zip(*operands)
    n_blocks = {sg.seq_len // sg.steps for sg in segs}
    assert len(n_blocks) == 1, segs
    return pl.pallas_call(
        kern,
        grid=(n_blocks.pop(),),
        in_specs=[_row_spec(sg, d_model, sg.x_batch_major) for sg in segs] + list(specs),
        out_specs=[spec for per_seg in out_specs for spec in per_seg],
        out_shape=[shape for per_seg in out_shapes for shape in per_seg],
        scratch_shapes=list(scratch),
        compiler_params=pltpu.CompilerParams(
            dimension_semantics=("arbitrary",), vmem_limit_bytes=VMEM_LIMIT_BYTES),
        name=name,
    )(*hs, *arrays)


def _per_segment(flat, segs, n_outs):
    return [flat[k * n_outs:(k + 1) * n_outs] for k in range(len(segs))]


S5_WEIGHTS = 9


def _s5_kernel(*refs, layer, ln_row, alpha, segs):
    xs, inits, weights, outs, _ = _split_refs(refs, segs, S5_WEIGHTS, 3)
    for x_ref, init_refs, out_refs, sg in zip(xs, inits, outs, segs):
        _s5_segment(x_ref, init_refs, weights, out_refs, sg, layer, ln_row, alpha)


def _s5_segment(x_ref, init_refs, weights, out_refs, sg, layer, ln_row, alpha):
    win_ref, drive_ref, read_ref, a2re_ref, a2im_ref, d_ref, wout_ref, g_ref, b_ref = weights
    out_ref, sre_ref, sim_ref = out_refs
    batch, sub_steps = sg.batch, sg.sub_steps
    d_model = out_ref.shape[1]
    n_chunks, cin2, cst2 = drive_ref.shape
    cin = cin2 // 2
    cst = cst2 // 2
    n_pairs = sub_steps // 2
    pair_rows = n_pairs * batch
    n_groups = batch // SUBLANES
    order = list(range(0, sub_steps, 2)) + list(range(1, sub_steps, 2))
    _init_carry((sre_ref, sim_ref), init_refs)
    ln_g, ln_b = g_ref[ln_row[0], ln_row[1]:ln_row[1] + 1, :], b_ref[ln_row[0], ln_row[1]:ln_row[1] + 1, :]
    d_skip = d_ref[layer:layer + 1, :]

    def rows_of(ref, rg, c):
        return ref[rg * SUBLANES:(rg + 1) * SUBLANES, c * cst:(c + 1) * cst]

    state = [[(rows_of(sre_ref, rg, c), rows_of(sim_ref, rg, c)) for rg in range(n_groups)]
             for c in range(n_chunks)]

    def pair_inputs(u, c):
        cols = slice(c * cin, (c + 1) * cin)
        return jnp.concatenate([u[:pair_rows, cols], u[pair_rows:, cols]], axis=1).astype(BF16)

    def scan_readout(up, drv, c):
        a_re = jnp.broadcast_to(a2re_ref[c:c + 1, :], (SUBLANES, cst))
        a_im = jnp.broadcast_to(a2im_ref[c:c + 1, :], (SUBLANES, cst))

        def step(carry, row):
            s_re, s_im = carry
            n_re = a_re * s_re - a_im * s_im + drv[row:row + SUBLANES, 0:cst]
            n_im = a_re * s_im + a_im * s_re + drv[row:row + SUBLANES, cst:cst2]
            return (n_re, n_im), (s_re, s_im)

        state[c], (st_re, st_im) = _slab_scan(step, state[c], n_pairs, batch)
        lhs = jnp.concatenate([st_re.astype(BF16), st_im.astype(BF16), up], axis=1)
        return _dot(lhs, read_ref[c])

    def finish(s, x, u, ys):
        y = jnp.concatenate([jnp.concatenate([yp[:, :cin] for yp in ys], axis=1),
                             jnp.concatenate([yp[:, cin:] for yp in ys], axis=1)], axis=0) + d_skip * u
        vg = _dot(jax.nn.gelu(y).astype(BF16), wout_ref[...])
        mix = vg[:, :d_model] * jax.nn.sigmoid(vg[:, d_model:])
        _store_rows(out_ref, s * sub_steps, sub_steps, batch, False,
                    _layer_norm(alpha * x + mix, ln_g, ln_b), order)

    pending = None
    for s in range(sg.steps // sub_steps):
        x = _load_rows(x_ref, s * sub_steps, sub_steps, batch, sg.x_batch_major, order)
        u = _dot(x.astype(BF16), win_ref[...])
        ys = []
        up_next = pair_inputs(u, 0)
        drv_next = _dot(up_next, drive_ref[0])
        for c in range(n_chunks):
            up, drv = up_next, drv_next
            if c + 1 < n_chunks:
                up_next = pair_inputs(u, c + 1)
                drv_next = _dot(up_next, drive_ref[c + 1])
            ys.append(scan_readout(up, drv, c))
            if c == 0 and pending is not None:
                finish(*pending)
        pending = (s, x, u, ys)
    finish(*pending)
    for c in range(n_chunks):
        for rg in range(n_groups):
            sre_ref[rg * SUBLANES:(rg + 1) * SUBLANES, c * cst:(c + 1) * cst] = state[c][rg][0]
            sim_ref[rg * SUBLANES:(rg + 1) * SUBLANES, c * cst:(c + 1) * cst] = state[c][rg][1]


def _s5_layer(hs, dims, inits, p, j, ln_row, alpha):
    d_model = hs[0].shape[-1]
    segs = _segments(hs, dims, inits, 2, False)
    assert all(sg.sub_steps % 2 == 0 for sg in segs), segs
    n_state = p["a2_re"].shape[1] * p["a2_re"].shape[2]
    weights = [
        _pick(p["s5_w_in"], {0: j}), _pick(p["s5_drive"], {0: j}), _pick(p["s5_read"], {0: j}),
        _pick(p["a2_re"], {0: j}), _pick(p["a2_im"], {0: j}), _whole(p["s5_d"]),
        _pick(p["s5_w_out"], {0: j}), _whole(p["ln_g"]), _whole(p["ln_b"])]
    assert len(weights) == S5_WEIGHTS
    kern = functools.partial(_s5_kernel, layer=j, ln_row=ln_row, alpha=alpha, segs=segs)
    state = lambda sg: jax.ShapeDtypeStruct((sg.batch, n_state), F32)
    flat = _call(
        kern, "s5_mixer_ln", segs, hs, inits, weights,
        [[_row_shape(sg, d_model, False), state(sg), state(sg)] for sg in segs],
        [[_row_spec(sg, d_model, False)] + [_carry_spec((sg.batch, n_state))] * 2 for sg in segs])
    return _per_segment(flat, segs, 3)


def _s5_params(lam_re, lam_im, log_step, b_re, b_im, c_re, c_im):
    n_layers, n_groups, n_state, gsz = b_re.shape
    gc = S5_CHUNK_GROUPS
    nc = n_groups // gc
    exact = lax.Precision.HIGHEST
    step = jnp.exp(log_step.astype(F32))[..., None]
    lr, li = lam_re.astype(F32), lam_im.astype(F32)
    mag = jnp.exp(lr * step)
    ab_re, ab_im = mag * jnp.cos(li * step), mag * jnp.sin(li * step)
    den = lr * lr + li * li
    nr, ni = ab_re - 1.0, ab_im
    q_re = ((nr * lr + ni * li) / den)[..., None]
    q_im = ((ni * lr - nr * li) / den)[..., None]
    br, bi = b_re.astype(F32), b_im.astype(F32)
    bb_re = q_re * br - q_im * bi
    bb_im = q_re * bi + q_im * br
    a2_re, a2_im = ab_re * ab_re - ab_im * ab_im, 2.0 * ab_re * ab_im
    lb_re = ab_re[..., None] * bb_re - ab_im[..., None] * bb_im
    lb_im = ab_re[..., None] * bb_im + ab_im[..., None] * bb_re
    cr, ci = c_re.astype(F32), c_im.astype(F32)

    def c_times(m_re, m_im):
        m_re, m_im = m_re[:, :, None, :], m_im[:, :, None, :]
        return cr * m_re - ci * m_im, cr * m_im + ci * m_re

    def re_c_times(m_re, m_im):
        return (jnp.einsum("ngop,ngpi->ngio", cr, m_re, precision=exact)
                - jnp.einsum("ngop,ngpi->ngio", ci, m_im, precision=exact))

    def compact(parts, row_axis, col_axis):
        t = jnp.stack([jnp.stack(row) for row in parts])
        t = t.reshape(2, 2, n_layers, nc, gc, t.shape[4], t.shape[5])
        t = t.transpose(2, 3, 0, 4, 5 + row_axis, 1, 5 + col_axis)
        return t.reshape(n_layers, nc, 2 * gc * t.shape[4], 2 * t.shape[6])

    def expand(x, group_rows, group_cols):
        c = x.shape[3] // 2
        tile = np.kron(np.eye(2), np.kron(np.ones((1, gc)), np.eye(c)))
        keep = group_rows[:, None] == group_cols[None, :]
        t = jnp.einsum("ncrj,jq->ncrq", x.astype(BF16), jnp.asarray(tile, BF16),
                       preferred_element_type=F32)
        return jnp.where(jnp.asarray(keep), t, 0.0).astype(BF16)

    group_of = lambda n, per_group: (np.arange(n) // per_group) % gc
    drive = expand(compact([[lb_re, lb_im], [bb_re, bb_im]], 1, 0),
                   group_of(2 * gc * gsz, gsz), group_of(2 * gc * n_state, n_state))
    cl_re, cl_im = c_times(ab_re, ab_im)
    cl2_re, cl2_im = c_times(a2_re, a2_im)
    k0 = re_c_times(bb_re, bb_im)
    k1 = re_c_times(lb_re, lb_im)
    read_rows = jnp.concatenate([compact([[cl_re, cl2_re], [-cl_im, -cl2_im]], 1, 0),
                                 compact([[k0, k1], [jnp.zeros_like(k0), k0]], 0, 1)], axis=2)
    read = expand(read_rows,
                  np.concatenate([group_of(2 * gc * n_state, n_state), group_of(2 * gc * gsz, gsz)]),
                  group_of(2 * gc * gsz, gsz))
    return dict(s5_drive=drive, s5_read=read, a2_re=a2_re.reshape(n_layers, nc, gc * n_state),
                a2_im=a2_im.reshape(n_layers, nc, gc * n_state))


RG_WEIGHTS = 9


def _softplus(z):
    return jnp.maximum(z, 0.0) + jnp.log1p(jnp.exp(-jnp.abs(z)))


def _rg_kernel(*refs, layer, ln_row, alpha, segs, n_carry):
    xs, inits, weights, outs, _ = _split_refs(refs, segs, RG_WEIGHTS, 1 + n_carry)
    for x_ref, init_refs, out_refs, sg in zip(xs, inits, outs, segs):
        _rg_segment(x_ref, init_refs, weights, out_refs, sg, layer, ln_row, alpha)


def _rg_segment(x_ref, init_refs, weights, out_refs, sg, layer, ln_row, alpha):
    win_ref, cw_ref, cb_ref, wg_ref, bg_ref, lam_ref, wout_ref, g_ref, b_ref = weights
    out_ref, hn_ref, hist_refs = out_refs[0], out_refs[1], out_refs[2:]
    batch, sub_steps = sg.batch, sg.sub_steps
    width = hn_ref.shape[1]
    blk = width // RG_BLOCKS
    sub_rows = sub_steps * batch
    n_groups = batch // SUBLANES
    _init_carry((hn_ref, *hist_refs), init_refs)
    ln_g, ln_b = g_ref[ln_row[0], ln_row[1]:ln_row[1] + 1, :], b_ref[ln_row[0], ln_row[1]:ln_row[1] + 1, :]
    cw = [cw_ref[layer, k:k + 1, :] for k in range(cw_ref.shape[1])]
    cb = cb_ref[layer:layer + 1, :]
    b_r, b_i = bg_ref[layer:layer + 1, :width], bg_ref[layer:layer + 1, width:]

    c_softplus = RG_C * _softplus(-lam_ref[layer:layer + 1, :])
    h_state = [[hn_ref[rg * SUBLANES:(rg + 1) * SUBLANES, n * blk:(n + 1) * blk] for rg in range(n_groups)]
               for n in range(RG_BLOCKS)]

    def x_branch(xb, n):
        return _dot(xb, win_ref[:, width + n * blk:width + (n + 1) * blk])

    def gate_block(xb, n, zx):
        cols = slice(n * blk, (n + 1) * blk)
        xc = _causal_conv(zx, hist_refs, n * blk, [w[:, cols] for w in cw], cb[:, cols], batch)
        gts = _dot(xc.astype(BF16), wg_ref[n])
        zx_next = x_branch(xb, n + 1) if n + 1 < RG_BLOCKS else None
        zg = _dot(xb, win_ref[:, cols])
        r = jax.nn.sigmoid(gts[:, :blk] + b_r[:, cols])
        ig = jax.nn.sigmoid(gts[:, blk:] + b_i[:, cols])
        neg_log_a = r * c_softplus[:, cols]
        a = jnp.exp(-neg_log_a)
        th = jnp.tanh(neg_log_a)
        v = th / (1.0 + th)
        root = jnp.where(v > 0.0, v * lax.rsqrt(v), 0.0)
        bb = (math.sqrt(2.0) * root) * (ig * xc)

        def step(h, row):
            h = a[row:row + SUBLANES, :] * h + bb[row:row + SUBLANES, :]
            return h, (h,)

        h_state[n], (hs,) = _slab_scan(step, h_state[n], sub_steps, batch)
        return (hs * jax.nn.gelu(zg)).astype(BF16), zx_next

    def finish(s, x, hg):
        y = _dot(hg, wout_ref[...])
        out_ref[s * sub_rows:(s + 1) * sub_rows, :] = _layer_norm(alpha * x + y, ln_g, ln_b)

    pending = None
    for s in range(sg.steps // sub_steps):
        x = x_ref[s * sub_rows:(s + 1) * sub_rows, :]
        xb = x.astype(BF16)
        blocks = []
        zx = x_branch(xb, 0)
        for n in range(RG_BLOCKS):
            hg, zx = gate_block(xb, n, zx)
            blocks.append(hg)
            if n == 0 and pending is not None:
                finish(*pending)
        pending = (s, x, jnp.concatenate(blocks, axis=1))
    finish(*pending)
    for n in range(RG_BLOCKS):
        for rg in range(n_groups):
            hn_ref[rg * SUBLANES:(rg + 1) * SUBLANES, n * blk:(n + 1) * blk] = h_state[n][rg]


def _rg_layer(hs, dims, inits, p, j, ln_row, alpha):
    d_model = hs[0].shape[-1]
    width = p["rg_w_out"].shape[1]
    n_taps = p["rg_conv_w"].shape[1]
    segs = _segments(hs, dims, inits, n_taps - 1, False)
    weights = [
        _pick(p["rg_w_in"], {0: j}), _whole(p["rg_conv_w"]), _whole(p["rg_conv_b"]),
        _pick(p["rg_w_gates"], {0: j}), _whole(p["rg_b_gates"]), _whole(p["rg_lam"]),
        _pick(p["rg_w_out"], {0: j}), _whole(p["ln_g"]), _whole(p["ln_b"])]
    assert len(weights) == RG_WEIGHTS
    n_carry = n_taps
    kern = functools.partial(_rg_kernel, layer=j, ln_row=ln_row, alpha=alpha, segs=segs, n_carry=n_carry)
    flat = _call(
        kern, "rglru_mixer_ln", segs, hs, inits, weights,
        [[_row_shape(sg, d_model, False)] + [jax.ShapeDtypeStruct((sg.batch, width), F32)] * n_carry
         for sg in segs],
        [[_row_spec(sg, d_model, False)] + [_carry_spec((sg.batch, width))] * n_carry for sg in segs])
    return _per_segment(flat, segs, 1 + n_carry)


FFN_WEIGHTS = 6


def _ffn_kernel(*refs, layer, ln_row, alpha, segs, n_hist):
    xs, inits, weights, outs, scratch = _split_refs(refs, segs, FFN_WEIGHTS, 1 + n_hist)
    for x_ref, init_refs, out_refs, act_scr, sg in zip(xs, inits, outs, scratch, segs):
        _ffn_segment(x_ref, init_refs, weights, out_refs, act_scr, sg, layer, ln_row, alpha)


def _ffn_segment(x_ref, init_refs, weights, out_refs, act_scr, sg, layer, ln_row, alpha):
    wup_ref, cw_ref, cb_ref, wd_ref, g_ref, b_ref = weights
    out_ref, hist_refs = out_refs[0], out_refs[1:]
    batch, sub_steps = sg.batch, sg.sub_steps
    ff = wd_ref.shape[0]
    n_taps = cw_ref.shape[1]
    sub_rows = sub_steps * batch
    _init_carry(hist_refs, init_refs)
    ln_g, ln_b = g_ref[ln_row[0], ln_row[1]:ln_row[1] + 1, :], b_ref[ln_row[0], ln_row[1]:ln_row[1] + 1, :]

    def finish(s, x):
        y = _dot(act_scr[s % 2], wd_ref[...])
        _store_rows(out_ref, s * sub_steps, sub_steps, batch, sg.out_batch_major,
                    _layer_norm(alpha * x + y, ln_g, ln_b))

    pending = None
    for s in range(sg.steps // sub_steps):
        x = x_ref[s * sub_rows:(s + 1) * sub_rows, :]
        xb = x.astype(BF16)
        for j in range(ff // FF_TILE):
            val, gate = [
                _causal_conv(_dot(xb, wup_ref[:, col:col + FF_TILE]), hist_refs, col,
                             [cw_ref[layer, k:k + 1, col:col + FF_TILE] for k in range(n_taps)],
                             cb_ref[layer:layer + 1, col:col + FF_TILE], batch)
                for col in (j * FF_TILE, ff + j * FF_TILE)]
            act_scr[s % 2, :, j * FF_TILE:(j + 1) * FF_TILE] = (jax.nn.gelu(gate) * val).astype(BF16)
            if j == 0 and pending is not None:
                finish(*pending)
        pending = (s, x)
    finish(*pending)


def _ffn_layer(hs, dims, inits, p, i, ln_row, alpha, out_batch_major):
    d_model = hs[0].shape[-1]
    ff = p["ffn_w_down"].shape[1]
    n_taps = p["ffn_conv_w"].shape[1]
    segs = _segments(hs, dims, inits, n_taps - 1, out_batch_major)
    weights = [
        _pick(p["ffn_w_up"], {0: i}), _whole(p["ffn_conv_w"]), _whole(p["ffn_conv_b"]),
        _pick(p["ffn_w_down"], {0: i}), _whole(p["ln_g"]), _whole(p["ln_b"])]
    assert len(weights) == FFN_WEIGHTS
    kern = functools.partial(_ffn_kernel, layer=i, ln_row=ln_row, alpha=alpha, segs=segs, n_hist=n_taps - 1)
    flat = _call(
        kern, "convffn_ln", segs, hs, inits, weights,
        [[_row_shape(sg, d_model, out_batch_major)]
         + [jax.ShapeDtypeStruct((sg.batch, 2 * ff), F32)] * (n_taps - 1) for sg in segs],
        [[_row_spec(sg, d_model, out_batch_major)] + [_carry_spec((sg.batch, 2 * ff))] * (n_taps - 1)
         for sg in segs],
        scratch=[pltpu.VMEM((2, sg.sub_steps * sg.batch, ff), BF16) for sg in segs])
    return _per_segment(flat, segs, n_taps)


def _trunks(xs, inits, p, alpha):
    dims = [x.shape[:2] for x in xs]
    depth = p["ffn_w_up"].shape[0]
    news = [dict(s5_re=[], s5_im=[], rg_h=[], rg_conv=[], ffn_conv=[]) for _ in xs]
    hs = list(xs)
    for i in range(depth):
        j = i // 2
        if i % 2 == 0:
            st = [() if init is None else (init["s5_re"][j], init["s5_im"][j]) for init in inits]
            outs = _s5_layer(hs, dims, st, p, j, (i, 0), alpha)
            for new, (_, sr, si) in zip(news, outs):
                new["s5_re"].append(sr)
                new["s5_im"].append(si)
        else:
            st = [() if init is None else (init["rg_h"][j], *init["rg_conv"][j]) for init in inits]
            outs = _rg_layer(hs, dims, st, p, j, (i, 0), alpha)
            for new, (_, hn, *hist) in zip(news, outs):
                new["rg_h"].append(hn)
                new["rg_conv"].append(hist)
        hs = [o[0] for o in outs]
        st = [() if init is None else tuple(init["ffn_conv"][i]) for init in inits]
        outs = _ffn_layer(hs, dims, st, p, i, (i, 1), alpha, out_batch_major=(i == depth - 1))
        for new, (_, *hist) in zip(news, outs):
            new["ffn_conv"].append(hist)
        hs = [o[0] for o in outs]
    return hs, news


def _carried(new):
    one = lambda a: _whole(a)
    return dict(s5_re=[one(a) for a in new["s5_re"]], s5_im=[one(a) for a in new["s5_im"]],
                rg_h=[one(a) for a in new["rg_h"]],
                rg_conv=[[one(a) for a in hist] for hist in new["rg_conv"]],
                ffn_conv=[[one(a) for a in hist] for hist in new["ffn_conv"]])


def _given(s5_re, s5_im, rg_h, rg_conv, ffn_conv):
    flat = lambda a: [_whole(a[k].reshape(a.shape[1], -1)) for k in range(a.shape[0])]

    def steps(a):
        t = a.transpose(0, 2, 1, 3)
        return [[_pick(t, {0: n, 1: k}) for k in range(t.shape[1])] for n in range(t.shape[0])]

    return dict(s5_re=flat(s5_re), s5_im=flat(s5_im), rg_h=[_pick(rg_h, {0: n}) for n in range(rg_h.shape[0])],
                rg_conv=steps(rg_conv), ffn_conv=steps(ffn_conv))


def _stack_states(new, s5_shape):
    s5 = lambda k: jnp.stack(new[k]).reshape((len(new[k]), new[k][0].shape[0]) + s5_shape)
    conv = lambda k: jnp.stack([jnp.stack(hist, axis=1) for hist in new[k]])
    return (s5("s5_re"), s5("s5_im"), jnp.stack(new["rg_h"]), conv("rg_conv"), conv("ffn_conv"))


def kernel(x_prompt, x_sample, state_s5_re, state_s5_im, state_rg_h, state_rg_conv, state_ffn_conv,
           meta_tokens, s5_w_in, s5_lam_re, s5_lam_im, s5_log_step, s5_b_re, s5_b_im, s5_c_re,
           s5_c_im, s5_d, s5_w_out, rg_w_in, rg_conv_w, rg_conv_b, rg_w_gates, rg_b_gates, rg_lam,
           rg_w_out, ffn_w_up, ffn_conv_w, ffn_conv_b, ffn_w_down, ln_g, ln_b):
    depth = ffn_w_up.shape[0]
    alpha = (2 * depth) ** 0.25
    bp = x_prompt.shape[0]
    dt = x_prompt.dtype

    p = dict(
        s5_w_in=s5_w_in.astype(BF16), s5_w_out=s5_w_out.astype(BF16), s5_d=s5_d,
        rg_w_in=rg_w_in.astype(BF16), rg_conv_w=rg_conv_w, rg_conv_b=rg_conv_b,
        rg_w_gates=rg_w_gates.astype(BF16), rg_b_gates=rg_b_gates, rg_lam=rg_lam,
        rg_w_out=rg_w_out.astype(BF16),
        ffn_w_up=ffn_w_up.astype(BF16), ffn_conv_w=ffn_conv_w, ffn_conv_b=ffn_conv_b,
        ffn_w_down=ffn_w_down.astype(BF16), ln_g=ln_g, ln_b=ln_b,
        **_s5_params(s5_lam_re, s5_lam_im, s5_log_step, s5_b_re, s5_b_im, s5_c_re, s5_c_im))
    s5_shape = state_s5_re.shape[2:]

    x_meta = jnp.broadcast_to(meta_tokens[None].astype(dt), (bp,) + meta_tokens.shape)
    given = _given(state_s5_re, state_s5_im, state_rg_h, state_rg_conv, state_ffn_conv)
    (_, y_sample), (meta_states, new_s) = _trunks([x_meta, x_sample], [None, given], p, alpha)
    (y_prompt,), (new_p,) = _trunks([x_prompt], [_carried(meta_states)], p, alpha)
    return (y_prompt, y_sample) + _stack_states(new_p, s5_shape) + _stack_states(new_s, s5_shape)
```

```python
import functools
import math
from typing import NamedTuple

import jax
import jax.numpy as jnp
import numpy as np
from jax import lax
from jax.experimental import pallas as pl
from jax.experimental.pallas import tpu as pltpu

RG_BLOCKS = 4
RG_C = 8.0
LN_EPS = 1e-5

SUBLANES = 8
S5_CHUNK_GROUPS = 8
MXU_COLS = 256
FF_TILE = MXU_COLS
BLOCK_ROWS = 1024
SUB_ROWS = 512
VMEM_LIMIT_BYTES = 56 * 1024 * 1024

BF16 = jnp.bfloat16
F32 = jnp.float32


class _Seg(NamedTuple):
    batch: int
    seq_len: int
    steps: int
    sub_steps: int
    n_init: int
    x_batch_major: bool
    out_batch_major: bool


def _dot(a, b):
    return jnp.dot(a, b, preferred_element_type=F32)


def _layer_norm(z, g, b):
    mu = jnp.mean(z, axis=-1, keepdims=True)
    zc = z - mu
    var = jnp.mean(zc * zc, axis=-1, keepdims=True)
    return zc * lax.rsqrt(var + LN_EPS) * g + b


def _largest_divisor(n, limit):
    return max(d for d in range(1, n + 1) if n % d == 0 and d <= limit)


def _segments(hs, dims, inits, min_steps, out_batch_major):
    segs = []
    for h, (batch, seq_len), init in zip(hs, dims, inits):
        steps = _largest_divisor(seq_len, max(BLOCK_ROWS // batch, min_steps))
        sub = _largest_divisor(steps, max(SUB_ROWS // batch, min_steps))
        assert sub >= min_steps and (len(hs) == 1 or steps == seq_len), (seq_len, batch, min_steps)
        segs.append(_Seg(batch, seq_len, steps, sub, len(init), h.ndim == 3, out_batch_major))
    return tuple(segs)


def _whole(a):
    return a, pl.BlockSpec(a.shape, lambda i: (0,) * a.ndim, pipeline_mode=pl.Buffered(1))


def _pick(a, fixed):
    block = tuple(None if d in fixed else n for d, n in enumerate(a.shape))
    index = tuple(fixed.get(d, 0) for d in range(a.ndim))
    return a, pl.BlockSpec(block, lambda i: index, pipeline_mode=pl.Buffered(1))


def _row_spec(sg, d_model, batch_major):
    if batch_major:
        return pl.BlockSpec((sg.batch, sg.steps, d_model), lambda i: (0, i, 0))
    return pl.BlockSpec((sg.steps * sg.batch, d_model), lambda i: (i, 0))


def _row_shape(sg, d_model, batch_major):
    shape = (sg.batch, sg.seq_len, d_model) if batch_major else (sg.seq_len * sg.batch, d_model)
    return jax.ShapeDtypeStruct(shape, F32)


def _carry_spec(shape):
    return pl.BlockSpec(shape, lambda i: (0,) * len(shape))


def _load_rows(x_ref, t0, n_steps, batch, batch_major, order=None):
    if batch_major:
        return jnp.concatenate([x_ref[:, t0 + t, :] for t in (order or range(n_steps))], axis=0)
    if order is None:
        return x_ref[t0 * batch:(t0 + n_steps) * batch, :]
    return jnp.concatenate([x_ref[(t0 + t) * batch:(t0 + t + 1) * batch, :] for t in order], axis=0)


def _store_rows(out_ref, t0, n_steps, batch, batch_major, val, order=None):
    if batch_major or order is not None:
        for i, t in enumerate(order or range(n_steps)):
            if batch_major:
                out_ref[:, t0 + t, :] = val[i * batch:(i + 1) * batch, :]
            else:
                out_ref[(t0 + t) * batch:(t0 + t + 1) * batch, :] = val[i * batch:(i + 1) * batch, :]
    else:
        out_ref[t0 * batch:(t0 + n_steps) * batch, :] = val


def _init_carry(carry_refs, init_refs):
    @pl.when(pl.program_id(0) == 0)
    def _():
        for k, ref in enumerate(carry_refs):
            ref[...] = init_refs[k][...] if init_refs else jnp.zeros_like(ref)


def _slab_scan(step_fn, carry, n_steps, batch):
    n_groups = batch // SUBLANES
    outs = [[None] * n_groups for _ in range(n_steps)]
    carry = list(carry)
    for t in range(n_steps):
        for rg in range(n_groups):
            row = t * batch + rg * SUBLANES
            carry[rg], outs[t][rg] = step_fn(carry[rg], row)
    n_out = len(outs[0][0])
    stacked = [jnp.concatenate([outs[t][rg][k] for t in range(n_steps) for rg in range(n_groups)], axis=0)
               for k in range(n_out)]
    return carry, stacked


def _causal_conv(up, hist_refs, col, cw, cb, batch):
    rows, width = up.shape
    n_hist = len(hist_refs)
    ext = jnp.concatenate([r[:, col:col + width] for r in hist_refs] + [up], axis=0)
    for k in range(n_hist):
        hist_refs[k][:, col:col + width] = ext[rows + k * batch:rows + (k + 1) * batch]
    y = cb + cw[0] * ext[0:rows]
    for k in range(1, n_hist + 1):
        y = y + cw[k] * ext[k * batch:k * batch + rows]
    return y


def _split_refs(refs, segs, n_weights, n_outs):
    xs, pos = refs[:len(segs)], len(segs)
    inits = []
    for sg in segs:
        inits.append(refs[pos:pos + sg.n_init])
        pos += sg.n_init
    weights, pos = refs[pos:pos + n_weights], pos + n_weights
    outs = []
    for _ in segs:
        outs.append(refs[pos:pos + n_outs])
        pos += n_outs
    return xs, inits, weights, outs, refs[pos:]


def _call(kern, name, segs, hs, inits, weights, out_shapes, out_specs, scratch=()):
    d_model = hs[0].shape[-1]
    operands = [pair for init in inits for pair in init] + list(weights)
    arrays, specs = zip(*operands)
    n_blocks = {sg.seq_len // sg.steps for sg in segs}
    assert len(n_blocks) == 1, segs
    return pl.pallas_call(
        kern,
        grid=(n_blocks.pop(),),
        in_specs=[_row_spec(sg, d_model, sg.x_batch_major) for sg in segs] + list(specs),
        out_specs=[spec for per_seg in out_specs for spec in per_seg],
        out_shape=[shape for per_seg in out_shapes for shape in per_seg],
        scratch_shapes=list(scratch),
        compiler_params=pltpu.CompilerParams(
            dimension_semantics=("arbitrary",), vmem_limit_bytes=VMEM_LIMIT_BYTES),
        name=name,
    )(*hs, *arrays)


def _per_segment(flat, segs, n_outs):
    return [flat[k * n_outs:(k + 1) * n_outs] for k in range(len(segs))]


S5_WEIGHTS = 9


def _s5_kernel(*refs, layer, ln_row, alpha, segs):
    xs, inits, weights, outs, _ = _split_refs(refs, segs, S5_WEIGHTS, 3)
    for x_ref, init_refs, out_refs, sg in zip(xs, inits, outs, segs):
        _s5_segment(x_ref, init_refs, weights, out_refs, sg, layer, ln_row, alpha)


def _s5_segment(x_ref, init_refs, weights, out_refs, sg, layer, ln_row, alpha):
    win_ref, drive_ref, read_ref, a2re_ref, a2im_ref, d_ref, wout_ref, g_ref, b_ref = weights
    out_ref, sre_ref, sim_ref = out_refs
    batch, sub_steps = sg.batch, sg.sub_steps
    d_model = out_ref.shape[1]
    n_chunks, cin2, cst2 = drive_ref.shape
    cin = cin2 // 2
    cst = cst2 // 2
    n_pairs = sub_steps // 2
    pair_rows = n_pairs * batch
    n_groups = batch // SUBLANES
    order = list(range(0, sub_steps, 2)) + list(range(1, sub_steps, 2))
    _init_carry((sre_ref, sim_ref), init_refs)
    ln_g, ln_b = g_ref[ln_row[0], ln_row[1]:ln_row[1] + 1, :], b_ref[ln_row[0], ln_row[1]:ln_row[1] + 1, :]
    d_skip = d_ref[layer:layer + 1, :]

    def rows_of(ref, rg, c):
        return ref[rg * SUBLANES:(rg + 1) * SUBLANES, c * cst:(c + 1) * cst]

    state = [[(rows_of(sre_ref, rg, c), rows_of(sim_ref, rg, c)) for rg in range(n_groups)]
             for c in range(n_chunks)]

    def pair_inputs(u, c):
        cols = slice(c * cin, (c + 1) * cin)
        return jnp.concatenate([u[:pair_rows, cols], u[pair_rows:, cols]], axis=1).astype(BF16)

    def scan_readout(up, drv, c):
        a_re = jnp.broadcast_to(a2re_ref[c:c + 1, :], (SUBLANES, cst))
        a_im = jnp.broadcast_to(a2im_ref[c:c + 1, :], (SUBLANES, cst))

        def step(carry, row):
            s_re, s_im = carry
            n_re = a_re * s_re - a_im * s_im + drv[row:row + SUBLANES, 0:cst]
            n_im = a_re * s_im + a_im * s_re + drv[row:row + SUBLANES, cst:cst2]
            return (n_re, n_im), (s_re, s_im)

        state[c], (st_re, st_im) = _slab_scan(step, state[c], n_pairs, batch)
        lhs = jnp.concatenate([st_re.astype(BF16), st_im.astype(BF16), up], axis=1)
        return _dot(lhs, read_ref[c])

    def finish(s, x, u, ys):
        for parity in range(2):
            rows = slice(parity * pair_rows, (parity + 1) * pair_rows)
            y = jnp.concatenate([yp[:, parity * cin:(parity + 1) * cin] for yp in ys], axis=1) + d_skip * u[rows]
            vg = _dot(jax.nn.gelu(y).astype(BF16), wout_ref[...])
            mix = vg[:, :d_model] * jax.nn.sigmoid(vg[:, d_model:])
            _store_rows(out_ref, s * sub_steps, n_pairs, batch, False,
                        _layer_norm(alpha * x[rows] + mix, ln_g, ln_b),
                        order[parity * n_pairs:(parity + 1) * n_pairs])

    pending = None
    for s in range(sg.steps // sub_steps):
        x = _load_rows(x_ref, s * sub_steps, sub_steps, batch, sg.x_batch_major, order)
        u = _dot(x.astype(BF16), win_ref[...])
        ys = []
        up_next = pair_inputs(u, 0)
        drv_next = _dot(up_next, drive_ref[0])
        for c in range(n_chunks):
            up, drv = up_next, drv_next
            if c + 1 < n_chunks:
                up_next = pair_inputs(u, c + 1)
                drv_next = _dot(up_next, drive_ref[c + 1])
            ys.append(scan_readout(up, drv, c))
            if c == 0 and pending is not None:
                finish(*pending)
        pending = (s, x, u, ys)
    finish(*pending)
    for c in range(n_chunks):
        for rg in range(n_groups):
            sre_ref[rg * SUBLANES:(rg + 1) * SUBLANES, c * cst:(c + 1) * cst] = state[c][rg][0]
            sim_ref[rg * SUBLANES:(rg + 1) * SUBLANES, c * cst:(c + 1) * cst] = state[c][rg][1]


def _s5_layer(hs, dims, inits, p, j, ln_row, alpha):
    d_model = hs[0].shape[-1]
    segs = _segments(hs, dims, inits, 2, False)
    assert all(sg.sub_steps % 2 == 0 for sg in segs), segs
    n_state = p["a2_re"].shape[1] * p["a2_re"].shape[2]
    weights = [
        _pick(p["s5_w_in"], {0: j}), _pick(p["s5_drive"], {0: j}), _pick(p["s5_read"], {0: j}),
        _pick(p["a2_re"], {0: j}), _pick(p["a2_im"], {0: j}), _whole(p["s5_d"]),
        _pick(p["s5_w_out"], {0: j}), _whole(p["ln_g"]), _whole(p["ln_b"])]
    assert len(weights) == S5_WEIGHTS
    kern = functools.partial(_s5_kernel, layer=j, ln_row=ln_row, alpha=alpha, segs=segs)
    state = lambda sg: jax.ShapeDtypeStruct((sg.batch, n_state), F32)
    flat = _call(
        kern, "s5_mixer_ln", segs, hs, inits, weights,
        [[_row_shape(sg, d_model, False), state(sg), state(sg)] for sg in segs],
        [[_row_spec(sg, d_model, False)] + [_carry_spec((sg.batch, n_state))] * 2 for sg in segs])
    return _per_segment(flat, segs, 3)


def _s5_params(lam_re, lam_im, log_step, b_re, b_im, c_re, c_im):
    n_layers, n_groups, n_state, gsz = b_re.shape
    gc = S5_CHUNK_GROUPS
    nc = n_groups // gc
    exact = lax.Precision.HIGHEST
    step = jnp.exp(log_step.astype(F32))[..., None]
    lr, li = lam_re.astype(F32), lam_im.astype(F32)
    mag = jnp.exp(lr * step)
    ab_re, ab_im = mag * jnp.cos(li * step), mag * jnp.sin(li * step)
    den = lr * lr + li * li
    nr, ni = ab_re - 1.0, ab_im
    q_re = ((nr * lr + ni * li) / den)[..., None]
    q_im = ((ni * lr - nr * li) / den)[..., None]
    br, bi = b_re.astype(F32), b_im.astype(F32)
    bb_re = q_re * br - q_im * bi
    bb_im = q_re * bi + q_im * br
    a2_re, a2_im = ab_re * ab_re - ab_im * ab_im, 2.0 * ab_re * ab_im
    lb_re = ab_re[..., None] * bb_re - ab_im[..., None] * bb_im
    lb_im = ab_re[..., None] * bb_im + ab_im[..., None] * bb_re
    cr, ci = c_re.astype(F32), c_im.astype(F32)

    def c_times(m_re, m_im):
        m_re, m_im = m_re[:, :, None, :], m_im[:, :, None, :]
        return cr * m_re - ci * m_im, cr * m_im + ci * m_re

    def re_c_times(m_re, m_im):
        return (jnp.einsum("ngop,ngpi->ngio", cr, m_re, precision=exact)
                - jnp.einsum("ngop,ngpi->ngio", ci, m_im, precision=exact))

    def compact(parts, row_axis, col_axis):
        t = jnp.stack([jnp.stack(row) for row in parts])
        t = t.reshape(2, 2, n_layers, nc, gc, t.shape[4], t.shape[5])
        t = t.transpose(2, 3, 0, 4, 5 + row_axis, 1, 5 + col_axis)
        return t.reshape(n_layers, nc, 2 * gc * t.shape[4], 2 * t.shape[6])

    def expand(x, group_rows, group_cols):
        c = x.shape[3] // 2
        tile = np.kron(np.eye(2), np.kron(np.ones((1, gc)), np.eye(c)))
        keep = group_rows[:, None] == group_cols[None, :]
        t = jnp.einsum("ncrj,jq->ncrq", x.astype(BF16), jnp.asarray(tile, BF16),
                       preferred_element_type=F32)
        return jnp.where(jnp.asarray(keep), t, 0.0).astype(BF16)

    group_of = lambda n, per_group: (np.arange(n) // per_group) % gc
    drive = expand(compact([[lb_re, lb_im], [bb_re, bb_im]], 1, 0),
                   group_of(2 * gc * gsz, gsz), group_of(2 * gc * n_state, n_state))
    cl_re, cl_im = c_times(ab_re, ab_im)
    cl2_re, cl2_im = c_times(a2_re, a2_im)
    k0 = re_c_times(bb_re, bb_im)
    k1 = re_c_times(lb_re, lb_im)
    read_rows = jnp.concatenate([compact([[cl_re, cl2_re], [-cl_im, -cl2_im]], 1, 0),
                                 compact([[k0, k1], [jnp.zeros_like(k0), k0]], 0, 1)], axis=2)
    read = expand(read_rows,
                  np.concatenate([group_of(2 * gc * n_state, n_state), group_of(2 * gc * gsz, gsz)]),
                  group_of(2 * gc * gsz, gsz))
    return dict(s5_drive=drive, s5_read=read, a2_re=a2_re.reshape(n_layers, nc, gc * n_state),
                a2_im=a2_im.reshape(n_layers, nc, gc * n_state))


RG_WEIGHTS = 9


def _softplus(z):
    return jnp.maximum(z, 0.0) + jnp.log1p(jnp.exp(-jnp.abs(z)))


def _rg_kernel(*refs, layer, ln_row, alpha, segs, n_carry):
    xs, inits, weights, outs, _ = _split_refs(refs, segs, RG_WEIGHTS, 1 + n_carry)
    for x_ref, init_refs, out_refs, sg in zip(xs, inits, outs, segs):
        _rg_segment(x_ref, init_refs, weights, out_refs, sg, layer, ln_row, alpha)


def _rg_segment(x_ref, init_refs, weights, out_refs, sg, layer, ln_row, alpha):
    win_ref, cw_ref, cb_ref, wg_ref, bg_ref, lam_ref, wout_ref, g_ref, b_ref = weights
    out_ref, hn_ref, hist_refs = out_refs[0], out_refs[1], out_refs[2:]
    batch, sub_steps = sg.batch, sg.sub_steps
    width = hn_ref.shape[1]
    blk = width // RG_BLOCKS
    sub_rows = sub_steps * batch
    n_groups = batch // SUBLANES
    _init_carry((hn_ref, *hist_refs), init_refs)
    ln_g, ln_b = g_ref[ln_row[0], ln_row[1]:ln_row[1] + 1, :], b_ref[ln_row[0], ln_row[1]:ln_row[1] + 1, :]
    cw = [cw_ref[layer, k:k + 1, :] for k in range(cw_ref.shape[1])]
    cb = cb_ref[layer:layer + 1, :]
    b_r, b_i = bg_ref[layer:layer + 1, :width], bg_ref[layer:layer + 1, width:]

    c_softplus = RG_C * _softplus(-lam_ref[layer:layer + 1, :])
    h_state = [[hn_ref[rg * SUBLANES:(rg + 1) * SUBLANES, n * blk:(n + 1) * blk] for rg in range(n_groups)]
               for n in range(RG_BLOCKS)]

    def x_branch(xb, n):
        return _dot(xb, win_ref[:, width + n * blk:width + (n + 1) * blk])

    def gate_block(xb, n, zx):
        cols = slice(n * blk, (n + 1) * blk)
        xc = _causal_conv(zx, hist_refs, n * blk, [w[:, cols] for w in cw], cb[:, cols], batch)
        gts = _dot(xc.astype(BF16), wg_ref[n])
        zx_next = x_branch(xb, n + 1) if n + 1 < RG_BLOCKS else None
        zg = _dot(xb, win_ref[:, cols])
        r = jax.nn.sigmoid(gts[:, :blk] + b_r[:, cols])
        ig = jax.nn.sigmoid(gts[:, blk:] + b_i[:, cols])
        neg_log_a = r * c_softplus[:, cols]
        a = jnp.exp(-neg_log_a)
        th = jnp.tanh(neg_log_a)
        v = th / (1.0 + th)
        root = jnp.where(v > 0.0, v * lax.rsqrt(v), 0.0)
        bb = (math.sqrt(2.0) * root) * (ig * xc)

        def step(h, row):
            h = a[row:row + SUBLANES, :] * h + bb[row:row + SUBLANES, :]
            return h, (h,)

        h_state[n], (hs,) = _slab_scan(step, h_state[n], sub_steps, batch)
        return (hs * jax.nn.gelu(zg)).astype(BF16), zx_next

    def finish(s, x, hg):
        y = _dot(hg, wout_ref[...])
        out_ref[s * sub_rows:(s + 1) * sub_rows, :] = _layer_norm(alpha * x + y, ln_g, ln_b)

    pending = None
    for s in range(sg.steps // sub_steps):
        x = x_ref[s * sub_rows:(s + 1) * sub_rows, :]
        xb = x.astype(BF16)
        blocks = []
        zx = x_branch(xb, 0)
        for n in range(RG_BLOCKS):
            hg, zx = gate_block(xb, n, zx)
            blocks.append(hg)
            if n == 0 and pending is not None:
                finish(*pending)
        pending = (s, x, jnp.concatenate(blocks, axis=1))
    finish(*pending)
    for n in range(RG_BLOCKS):
        for rg in range(n_groups):
            hn_ref[rg * SUBLANES:(rg + 1) * SUBLANES, n * blk:(n + 1) * blk] = h_state[n][rg]


def _rg_layer(hs, dims, inits, p, j, ln_row, alpha):
    d_model = hs[0].shape[-1]
    width = p["rg_w_out"].shape[1]
    n_taps = p["rg_conv_w"].shape[1]
    segs = _segments(hs, dims, inits, n_taps - 1, False)
    weights = [
        _pick(p["rg_w_in"], {0: j}), _whole(p["rg_conv_w"]), _whole(p["rg_conv_b"]),
        _pick(p["rg_w_gates"], {0: j}), _whole(p["rg_b_gates"]), _whole(p["rg_lam"]),
        _pick(p["rg_w_out"], {0: j}), _whole(p["ln_g"]), _whole(p["ln_b"])]
    assert len(weights) == RG_WEIGHTS
    n_carry = n_taps
    kern = functools.partial(_rg_kernel, layer=j, ln_row=ln_row, alpha=alpha, segs=segs, n_carry=n_carry)
    flat = _call(
        kern, "rglru_mixer_ln", segs, hs, inits, weights,
        [[_row_shape(sg, d_model, False)] + [jax.ShapeDtypeStruct((sg.batch, width), F32)] * n_carry
         for sg in segs],
        [[_row_spec(sg, d_model, False)] + [_carry_spec((sg.batch, width))] * n_carry for sg in segs])
    return _per_segment(flat, segs, 1 + n_carry)


FFN_WEIGHTS = 6


def _ffn_kernel(*refs, layer, ln_row, alpha, segs, n_hist):
    xs, inits, weights, outs, scratch = _split_refs(refs, segs, FFN_WEIGHTS, 1 + n_hist)
    for x_ref, init_refs, out_refs, act_scr, sg in zip(xs, inits, outs, scratch, segs):
        _ffn_segment(x_ref, init_refs, weights, out_refs, act_scr, sg, layer, ln_row, alpha)


def _ffn_segment(x_ref, init_refs, weights, out_refs, act_scr, sg, layer, ln_row, alpha):
    wup_ref, cw_ref, cb_ref, wd_ref, g_ref, b_ref = weights
    out_ref, hist_refs = out_refs[0], out_refs[1:]
    batch, sub_steps = sg.batch, sg.sub_steps
    ff = wd_ref.shape[0]
    n_taps = cw_ref.shape[1]
    sub_rows = sub_steps * batch
    _init_carry(hist_refs, init_refs)
    ln_g, ln_b = g_ref[ln_row[0], ln_row[1]:ln_row[1] + 1, :], b_ref[ln_row[0], ln_row[1]:ln_row[1] + 1, :]

    def finish(s, x):
        half_steps = sub_steps // 2 if sub_steps % 2 == 0 else sub_steps
        for t0 in range(0, sub_steps, half_steps):
            rows = slice(t0 * batch, (t0 + half_steps) * batch)
            y = _dot(act_scr[s % 2, rows, :], wd_ref[...])
            _store_rows(out_ref, s * sub_steps + t0, half_steps, batch, sg.out_batch_major,
                        _layer_norm(alpha * x[rows] + y, ln_g, ln_b))

    pending = None
    for s in range(sg.steps // sub_steps):
        x = x_ref[s * sub_rows:(s + 1) * sub_rows, :]
        xb = x.astype(BF16)
        for j in range(ff // FF_TILE):
            val, gate = [
                _causal_conv(_dot(xb, wup_ref[:, col:col + FF_TILE]), hist_refs, col,
                             [cw_ref[layer, k:k + 1, col:col + FF_TILE] for k in range(n_taps)],
                             cb_ref[layer:layer + 1, col:col + FF_TILE], batch)
                for col in (j * FF_TILE, ff + j * FF_TILE)]
            act_scr[s % 2, :, j * FF_TILE:(j + 1) * FF_TILE] = (jax.nn.gelu(gate) * val).astype(BF16)
            if j == 0 and pending is not None:
                finish(*pending)
        pending = (s, x)
    finish(*pending)


def _ffn_layer(hs, dims, inits, p, i, ln_row, alpha, out_batch_major):
    d_model = hs[0].shape[-1]
    ff = p["ffn_w_down"].shape[1]
    n_taps = p["ffn_conv_w"].shape[1]
    segs = _segments(hs, dims, inits, n_taps - 1, out_batch_major)
    weights = [
        _pick(p["ffn_w_up"], {0: i}), _whole(p["ffn_conv_w"]), _whole(p["ffn_conv_b"]),
        _pick(p["ffn_w_down"], {0: i}), _whole(p["ln_g"]), _whole(p["ln_b"])]
    assert len(weights) == FFN_WEIGHTS
    kern = functools.partial(_ffn_kernel, layer=i, ln_row=ln_row, alpha=alpha, segs=segs, n_hist=n_taps - 1)
    flat = _call(
        kern, "convffn_ln", segs, hs, inits, weights,
        [[_row_shape(sg, d_model, out_batch_major)]
         + [jax.ShapeDtypeStruct((sg.batch, 2 * ff), F32)] * (n_taps - 1) for sg in segs],
        [[_row_spec(sg, d_model, out_batch_major)] + [_carry_spec((sg.batch, 2 * ff))] * (n_taps - 1)
         for sg in segs],
        scratch=[pltpu.VMEM((2, sg.sub_steps * sg.batch, ff), BF16) for sg in segs])
    return _per_segment(flat, segs, n_taps)


def _trunks(xs, inits, p, alpha):
    dims = [x.shape[:2] for x in xs]
    depth = p["ffn_w_up"].shape[0]
    news = [dict(s5_re=[], s5_im=[], rg_h=[], rg_conv=[], ffn_conv=[]) for _ in xs]
    hs = list(xs)
    for i in range(depth):
        j = i // 2
        if i % 2 == 0:
            st = [() if init is None else (init["s5_re"][j], init["s5_im"][j]) for init in inits]
            outs = _s5_layer(hs, dims, st, p, j, (i, 0), alpha)
            for new, (_, sr, si) in zip(news, outs):
                new["s5_re"].append(sr)
                new["s5_im"].append(si)
        else:
            st = [() if init is None else (init["rg_h"][j], *init["rg_conv"][j]) for init in inits]
            outs = _rg_layer(hs, dims, st, p, j, (i, 0), alpha)
            for new, (_, hn, *hist) in zip(news, outs):
                new["rg_h"].append(hn)
                new["rg_conv"].append(hist)
        hs = [o[0] for o in outs]
        st = [() if init is None else tuple(init["ffn_conv"][i]) for init in inits]
        outs = _ffn_layer(hs, dims, st, p, i, (i, 1), alpha, out_batch_major=(i == depth - 1))
        for new, (_, *hist) in zip(news, outs):
            new["ffn_conv"].append(hist)
        hs = [o[0] for o in outs]
    return hs, news


def _carried(new):
    one = lambda a: _whole(a)
    return dict(s5_re=[one(a) for a in new["s5_re"]], s5_im=[one(a) for a in new["s5_im"]],
                rg_h=[one(a) for a in new["rg_h"]],
                rg_conv=[[one(a) for a in hist] for hist in new["rg_conv"]],
                ffn_conv=[[one(a) for a in hist] for hist in new["ffn_conv"]])


def _given(s5_re, s5_im, rg_h, rg_conv, ffn_conv):
    flat = lambda a: [_whole(a[k].reshape(a.shape[1], -1)) for k in range(a.shape[0])]

    def steps(a):
        t = a.transpose(0, 2, 1, 3)
        return [[_pick(t, {0: n, 1: k}) for k in range(t.shape[1])] for n in range(t.shape[0])]

    return dict(s5_re=flat(s5_re), s5_im=flat(s5_im), rg_h=[_pick(rg_h, {0: n}) for n in range(rg_h.shape[0])],
                rg_conv=steps(rg_conv), ffn_conv=steps(ffn_conv))


def _stack_states(new, s5_shape):
    s5 = lambda k: jnp.stack(new[k]).reshape((len(new[k]), new[k][0].shape[0]) + s5_shape)
    conv = lambda k: jnp.stack([jnp.stack(hist, axis=1) for hist in new[k]])
    return (s5("s5_re"), s5("s5_im"), jnp.stack(new["rg_h"]), conv("rg_conv"), conv("ffn_conv"))


def kernel(x_prompt, x_sample, state_s5_re, state_s5_im, state_rg_h, state_rg_conv, state_ffn_conv,
           meta_tokens, s5_w_in, s5_lam_re, s5_lam_im, s5_log_step, s5_b_re, s5_b_im, s5_c_re,
           s5_c_im, s5_d, s5_w_out, rg_w_in, rg_conv_w, rg_conv_b, rg_w_gates, rg_b_gates, rg_lam,
           rg_w_out, ffn_w_up, ffn_conv_w, ffn_conv_b, ffn_w_down, ln_g, ln_b):
    depth = ffn_w_up.shape[0]
    alpha = (2 * depth) ** 0.25
    bp = x_prompt.shape[0]
    dt = x_prompt.dtype

    p = dict(
        s5_w_in=s5_w_in.astype(BF16), s5_w_out=s5_w_out.astype(BF16), s5_d=s5_d,
        rg_w_in=rg_w_in.astype(BF16), rg_conv_w=rg_conv_w, rg_conv_b=rg_conv_b,
        rg_w_gates=rg_w_gates.astype(BF16), rg_b_gates=rg_b_gates, rg_lam=rg_lam,
        rg_w_out=rg_w_out.astype(BF16),
        ffn_w_up=ffn_w_up.astype(BF16), ffn_conv_w=ffn_conv_w, ffn_conv_b=ffn_conv_b,
        ffn_w_down=ffn_w_down.astype(BF16), ln_g=ln_g, ln_b=ln_b,
        **_s5_params(s5_lam_re, s5_lam_im, s5_log_step, s5_b_re, s5_b_im, s5_c_re, s5_c_im))
    s5_shape = state_s5_re.shape[2:]

    x_meta = jnp.broadcast_to(meta_tokens[None].astype(dt), (bp,) + meta_tokens.shape)
    given = _given(state_s5_re, state_s5_im, state_rg_h, state_rg_conv, state_ffn_conv)
    (_, y_sample), (meta_states, new_s) = _trunks([x_meta, x_sample], [None, given], p, alpha)
    (y_prompt,), (new_p,) = _trunks([x_prompt], [_carried(meta_states)], p, alpha)
    return (y_prompt, y_sample) + _stack_states(new_p, s5_shape) + _stack_states(new_s, s5_shape)
```

```python
import functools
import math
from typing import NamedTuple

import jax
import jax.numpy as jnp
import numpy as np
from jax import lax
from jax.experimental import pallas as pl
from jax.experimental.pallas import tpu as pltpu

RG_BLOCKS = 4
RG_C = 8.0
LN_EPS = 1e-5

SUBLANES = 8
S5_CHUNK_GROUPS = 8
MXU_COLS = 256
FF_TILE = MXU_COLS
BLOCK_ROWS = 1024
SUB_ROWS = 512
RG_SUB_ROWS = BLOCK_ROWS
VMEM_LIMIT_BYTES = 56 * 1024 * 1024

BF16 = jnp.bfloat16
F32 = jnp.float32


class _Seg(NamedTuple):
    batch: int
    seq_len: int
    steps: int
    sub_steps: int
    n_init: int
    x_batch_major: bool
    out_batch_major: bool


def _dot(a, b):
    return jnp.dot(a, b, preferred_element_type=F32)


def _layer_norm(z, g, b):
    mu = jnp.mean(z, axis=-1, keepdims=True)
    zc = z - mu
    var = jnp.mean(zc * zc, axis=-1, keepdims=True)
    return zc * lax.rsqrt(var + LN_EPS) * g + b


def _largest_divisor(n, limit):
    return max(d for d in range(1, n + 1) if n % d == 0 and d <= limit)


def _segments(hs, dims, inits, min_steps, out_batch_major, sub_rows=SUB_ROWS):
    segs = []
    for h, (batch, seq_len), init in zip(hs, dims, inits):
        steps = _largest_divisor(seq_len, max(BLOCK_ROWS // batch, min_steps))
        sub = _largest_divisor(steps, max(sub_rows // batch, min_steps))
        assert sub >= min_steps and (len(hs) == 1 or steps == seq_len), (seq_len, batch, min_steps)
        segs.append(_Seg(batch, seq_len, steps, sub, len(init), h.ndim == 3, out_batch_major))
    return tuple(segs)


def _whole(a):
    return a, pl.BlockSpec(a.shape, lambda i: (0,) * a.ndim, pipeline_mode=pl.Buffered(1))


def _pick(a, fixed):
    block = tuple(None if d in fixed else n for d, n in enumerate(a.shape))
    index = tuple(fixed.get(d, 0) for d in range(a.ndim))
    return a, pl.BlockSpec(block, lambda i: index, pipeline_mode=pl.Buffered(1))


def _row_spec(sg, d_model, batch_major):
    if batch_major:
        return pl.BlockSpec((sg.batch, sg.steps, d_model), lambda i: (0, i, 0))
    return pl.BlockSpec((sg.steps * sg.batch, d_model), lambda i: (i, 0))


def _row_shape(sg, d_model, batch_major):
    shape = (sg.batch, sg.seq_len, d_model) if batch_major else (sg.seq_len * sg.batch, d_model)
    return jax.ShapeDtypeStruct(shape, F32)


def _carry_spec(shape):
    return pl.BlockSpec(shape, lambda i: (0,) * len(shape))


def _load_rows(x_ref, t0, n_steps, batch, batch_major, order=None):
    if batch_major:
        return jnp.concatenate([x_ref[:, t0 + t, :] for t in (order or range(n_steps))], axis=0)
    if order is None:
        return x_ref[t0 * batch:(t0 + n_steps) * batch, :]
    return jnp.concatenate([x_ref[(t0 + t) * batch:(t0 + t + 1) * batch, :] for t in order], axis=0)


def _store_rows(out_ref, t0, n_steps, batch, batch_major, val, order=None):
    if batch_major or order is not None:
        for i, t in enumerate(order or range(n_steps)):
            if batch_major:
                out_ref[:, t0 + t, :] = val[i * batch:(i + 1) * batch, :]
            else:
                out_ref[(t0 + t) * batch:(t0 + t + 1) * batch, :] = val[i * batch:(i + 1) * batch, :]
    else:
        out_ref[t0 * batch:(t0 + n_steps) * batch, :] = val


def _init_carry(carry_refs, init_refs):
    @pl.when(pl.program_id(0) == 0)
    def _():
        for k, ref in enumerate(carry_refs):
            ref[...] = init_refs[k][...] if init_refs else jnp.zeros_like(ref)


def _slab_scan(step_fn, carry, n_steps, batch):
    n_groups = batch // SUBLANES
    outs = [[None] * n_groups for _ in range(n_steps)]
    carry = list(carry)
    for t in range(n_steps):
        for rg in range(n_groups):
            row = t * batch + rg * SUBLANES
            carry[rg], outs[t][rg] = step_fn(carry[rg], row)
    n_out = len(outs[0][0])
    stacked = [jnp.concatenate([outs[t][rg][k] for t in range(n_steps) for rg in range(n_groups)], axis=0)
               for k in range(n_out)]
    return carry, stacked


def _causal_conv(up, hist_refs, col, cw, cb, batch):
    rows, width = up.shape
    n_hist = len(hist_refs)
    ext = jnp.concatenate([r[:, col:col + width] for r in hist_refs] + [up], axis=0)
    for k in range(n_hist):
        hist_refs[k][:, col:col + width] = ext[rows + k * batch:rows + (k + 1) * batch]
    y = cb + cw[0] * ext[0:rows]
    for k in range(1, n_hist + 1):
        y = y + cw[k] * ext[k * batch:k * batch + rows]
    return y


def _split_refs(refs, segs, n_weights, n_outs):
    xs, pos = refs[:len(segs)], len(segs)
    inits = []
    for sg in segs:
        inits.append(refs[pos:pos + sg.n_init])
        pos += sg.n_init
    weights, pos = refs[pos:pos + n_weights], pos + n_weights
    outs = []
    for _ in segs:
        outs.append(refs[pos:pos + n_outs])
        pos += n_outs
    return xs, inits, weights, outs, refs[pos:]


def _call(kern, name, segs, hs, inits, weights, out_shapes, out_specs, scratch=()):
    d_model = hs[0].shape[-1]
    operands = [pair for init in inits for pair in init] + list(weights)
    arrays, specs = zip(*operands)
    n_blocks = {sg.seq_len // sg.steps for sg in segs}
    assert len(n_blocks) == 1, segs
    return pl.pallas_call(
        kern,
        grid=(n_blocks.pop(),),
        in_specs=[_row_spec(sg, d_model, sg.x_batch_major) for sg in segs] + list(specs),
        out_specs=[spec for per_seg in out_specs for spec in per_seg],
        out_shape=[shape for per_seg in out_shapes for shape in per_seg],
        scratch_shapes=list(scratch),
        compiler_params=pltpu.CompilerParams(
            dimension_semantics=("arbitrary",), vmem_limit_bytes=VMEM_LIMIT_BYTES),
        name=name,
    )(*hs, *arrays)


def _per_segment(flat, segs, n_outs):
    return [flat[k * n_outs:(k + 1) * n_outs] for k in range(len(segs))]


S5_WEIGHTS = 9


def _s5_kernel(*refs, layer, ln_row, alpha, segs):
    xs, inits, weights, outs, _ = _split_refs(refs, segs, S5_WEIGHTS, 3)
    for x_ref, init_refs, out_refs, sg in zip(xs, inits, outs, segs):
        _s5_segment(x_ref, init_refs, weights, out_refs, sg, layer, ln_row, alpha)


def _s5_segment(x_ref, init_refs, weights, out_refs, sg, layer, ln_row, alpha):
    win_ref, drive_ref, read_ref, a2re_ref, a2im_ref, d_ref, wout_ref, g_ref, b_ref = weights
    out_ref, sre_ref, sim_ref = out_refs
    batch, sub_steps = sg.batch, sg.sub_steps
    d_model = out_ref.shape[1]
    n_chunks, cin2, cst2 = drive_ref.shape
    cin = cin2 // 2
    cst = cst2 // 2
    n_pairs = sub_steps // 2
    pair_rows = n_pairs * batch
    n_groups = batch // SUBLANES
    order = list(range(0, sub_steps, 2)) + list(range(1, sub_steps, 2))
    _init_carry((sre_ref, sim_ref), init_refs)
    ln_g, ln_b = g_ref[ln_row[0], ln_row[1]:ln_row[1] + 1, :], b_ref[ln_row[0], ln_row[1]:ln_row[1] + 1, :]
    d_skip = d_ref[layer:layer + 1, :]

    def rows_of(ref, rg, c):
        return ref[rg * SUBLANES:(rg + 1) * SUBLANES, c * cst:(c + 1) * cst]

    state = [[(rows_of(sre_ref, rg, c), rows_of(sim_ref, rg, c)) for rg in range(n_groups)]
             for c in range(n_chunks)]

    def pair_inputs(u, c):
        cols = slice(c * cin, (c + 1) * cin)
        return jnp.concatenate([u[:pair_rows, cols], u[pair_rows:, cols]], axis=1).astype(BF16)

    def scan_readout(up, drv, c):
        a_re = jnp.broadcast_to(a2re_ref[c:c + 1, :], (SUBLANES, cst))
        a_im = jnp.broadcast_to(a2im_ref[c:c + 1, :], (SUBLANES, cst))

        def step(carry, row):
            s_re, s_im = carry
            n_re = a_re * s_re - a_im * s_im + drv[row:row + SUBLANES, 0:cst]
            n_im = a_re * s_im + a_im * s_re + drv[row:row + SUBLANES, cst:cst2]
            return (n_re, n_im), (s_re, s_im)

        state[c], (st_re, st_im) = _slab_scan(step, state[c], n_pairs, batch)
        lhs = jnp.concatenate([st_re.astype(BF16), st_im.astype(BF16), up], axis=1)
        return _dot(lhs, read_ref[c])

    def finish(s, x, u, ys):
        for parity in range(2):
            rows = slice(parity * pair_rows, (parity + 1) * pair_rows)
            y = jnp.concatenate([yp[:, parity * cin:(parity + 1) * cin] for yp in ys], axis=1) + d_skip * u[rows]
            vg = _dot(jax.nn.gelu(y).astype(BF16), wout_ref[...])
            mix = vg[:, :d_model] * jax.nn.sigmoid(vg[:, d_model:])
            _store_rows(out_ref, s * sub_steps, n_pairs, batch, False,
                        _layer_norm(alpha * x[rows] + mix, ln_g, ln_b),
                        order[parity * n_pairs:(parity + 1) * n_pairs])

    pending = None
    for s in range(sg.steps // sub_steps):
        x = _load_rows(x_ref, s * sub_steps, sub_steps, batch, sg.x_batch_major, order)
        u = _dot(x.astype(BF16), win_ref[...])
        ys = []
        up_next = pair_inputs(u, 0)
        drv_next = _dot(up_next, drive_ref[0])
        for c in range(n_chunks):
            up, drv = up_next, drv_next
            if c + 1 < n_chunks:
                up_next = pair_inputs(u, c + 1)
                drv_next = _dot(up_next, drive_ref[c + 1])
            ys.append(scan_readout(up, drv, c))
            if c == 0 and pending is not None:
                finish(*pending)
        pending = (s, x, u, ys)
    finish(*pending)
    for c in range(n_chunks):
        for rg in range(n_groups):
            sre_ref[rg * SUBLANES:(rg + 1) * SUBLANES, c * cst:(c + 1) * cst] = state[c][rg][0]
            sim_ref[rg * SUBLANES:(rg + 1) * SUBLANES, c * cst:(c + 1) * cst] = state[c][rg][1]


def _s5_layer(hs, dims, inits, p, j, ln_row, alpha):
    d_model = hs[0].shape[-1]
    segs = _segments(hs, dims, inits, 2, False)
    assert all(sg.sub_steps % 2 == 0 for sg in segs), segs
    n_state = p["a2_re"].shape[1] * p["a2_re"].shape[2]
    weights = [
        _pick(p["s5_w_in"], {0: j}), _pick(p["s5_drive"], {0: j}), _pick(p["s5_read"], {0: j}),
        _pick(p["a2_re"], {0: j}), _pick(p["a2_im"], {0: j}), _whole(p["s5_d"]),
        _pick(p["s5_w_out"], {0: j}), _whole(p["ln_g"]), _whole(p["ln_b"])]
    assert len(weights) == S5_WEIGHTS
    kern = functools.partial(_s5_kernel, layer=j, ln_row=ln_row, alpha=alpha, segs=segs)
    state = lambda sg: jax.ShapeDtypeStruct((sg.batch, n_state), F32)
    flat = _call(
        kern, "s5_mixer_ln", segs, hs, inits, weights,
        [[_row_shape(sg, d_model, False), state(sg), state(sg)] for sg in segs],
        [[_row_spec(sg, d_model, False)] + [_carry_spec((sg.batch, n_state))] * 2 for sg in segs])
    return _per_segment(flat, segs, 3)


def _s5_params(lam_re, lam_im, log_step, b_re, b_im, c_re, c_im):
    n_layers, n_groups, n_state, gsz = b_re.shape
    gc = S5_CHUNK_GROUPS
    nc = n_groups // gc
    exact = lax.Precision.HIGHEST
    step = jnp.exp(log_step.astype(F32))[..., None]
    lr, li = lam_re.astype(F32), lam_im.astype(F32)
    mag = jnp.exp(lr * step)
    ab_re, ab_im = mag * jnp.cos(li * step), mag * jnp.sin(li * step)
    den = lr * lr + li * li
    nr, ni = ab_re - 1.0, ab_im
    q_re = ((nr * lr + ni * li) / den)[..., None]
    q_im = ((ni * lr - nr * li) / den)[..., None]
    br, bi = b_re.astype(F32), b_im.astype(F32)
    bb_re = q_re * br - q_im * bi
    bb_im = q_re * bi + q_im * br
    a2_re, a2_im = ab_re * ab_re - ab_im * ab_im, 2.0 * ab_re * ab_im
    lb_re = ab_re[..., None] * bb_re - ab_im[..., None] * bb_im
    lb_im = ab_re[..., None] * bb_im + ab_im[..., None] * bb_re
    cr, ci = c_re.astype(F32), c_im.astype(F32)

    def c_times(m_re, m_im):
        m_re, m_im = m_re[:, :, None, :], m_im[:, :, None, :]
        return cr * m_re - ci * m_im, cr * m_im + ci * m_re

    def re_c_times(m_re, m_im):
        return (jnp.einsum("ngop,ngpi->ngio", cr, m_re, precision=exact)
                - jnp.einsum("ngop,ngpi->ngio", ci, m_im, precision=exact))

    def compact(parts, row_axis, col_axis):
        t = jnp.stack([jnp.stack(row) for row in parts])
        t = t.reshape(2, 2, n_layers, nc, gc, t.shape[4], t.shape[5])
        t = t.transpose(2, 3, 0, 4, 5 + row_axis, 1, 5 + col_axis)
        return t.reshape(n_layers, nc, 2 * gc * t.shape[4], 2 * t.shape[6])

    def expand(x, group_rows, group_cols):
        c = x.shape[3] // 2
        tile = np.kron(np.eye(2), np.kron(np.ones((1, gc)), np.eye(c)))
        keep = group_rows[:, None] == group_cols[None, :]
        t = jnp.einsum("ncrj,jq->ncrq", x.astype(BF16), jnp.asarray(tile, BF16),
                       preferred_element_type=F32)
        return jnp.where(jnp.asarray(keep), t, 0.0).astype(BF16)

    group_of = lambda n, per_group: (np.arange(n) // per_group) % gc
    drive = expand(compact([[lb_re, lb_im], [bb_re, bb_im]], 1, 0),
                   group_of(2 * gc * gsz, gsz), group_of(2 * gc * n_state, n_state))
    cl_re, cl_im = c_times(ab_re, ab_im)
    cl2_re, cl2_im = c_times(a2_re, a2_im)
    k0 = re_c_times(bb_re, bb_im)
    k1 = re_c_times(lb_re, lb_im)
    read_rows = jnp.concatenate([compact([[cl_re, cl2_re], [-cl_im, -cl2_im]], 1, 0),
                                 compact([[k0, k1], [jnp.zeros_like(k0), k0]], 0, 1)], axis=2)
    read = expand(read_rows,
                  np.concatenate([group_of(2 * gc * n_state, n_state), group_of(2 * gc * gsz, gsz)]),
                  group_of(2 * gc * gsz, gsz))
    return dict(s5_drive=drive, s5_read=read, a2_re=a2_re.reshape(n_layers, nc, gc * n_state),
                a2_im=a2_im.reshape(n_layers, nc, gc * n_state))


RG_WEIGHTS = 9


def _softplus(z):
    return jnp.maximum(z, 0.0) + jnp.log1p(jnp.exp(-jnp.abs(z)))


def _rg_kernel(*refs, layer, ln_row, alpha, segs, n_carry):
    xs, inits, weights, outs, _ = _split_refs(refs, segs, RG_WEIGHTS, 1 + n_carry)
    for x_ref, init_refs, out_refs, sg in zip(xs, inits, outs, segs):
        _rg_segment(x_ref, init_refs, weights, out_refs, sg, layer, ln_row, alpha)


def _rg_segment(x_ref, init_refs, weights, out_refs, sg, layer, ln_row, alpha):
    win_ref, cw_ref, cb_ref, wg_ref, bg_ref, lam_ref, wout_ref, g_ref, b_ref = weights
    out_ref, hn_ref, hist_refs = out_refs[0], out_refs[1], out_refs[2:]
    batch, sub_steps = sg.batch, sg.sub_steps
    width = hn_ref.shape[1]
    blk = width // RG_BLOCKS
    sub_rows = sub_steps * batch
    n_groups = batch // SUBLANES
    _init_carry((hn_ref, *hist_refs), init_refs)
    ln_g, ln_b = g_ref[ln_row[0], ln_row[1]:ln_row[1] + 1, :], b_ref[ln_row[0], ln_row[1]:ln_row[1] + 1, :]
    cw = [cw_ref[layer, k:k + 1, :] for k in range(cw_ref.shape[1])]
    cb = cb_ref[layer:layer + 1, :]
    b_r, b_i = bg_ref[layer:layer + 1, :width], bg_ref[layer:layer + 1, width:]

    c_softplus = RG_C * _softplus(-lam_ref[layer:layer + 1, :])
    h_state = [[hn_ref[rg * SUBLANES:(rg + 1) * SUBLANES, n * blk:(n + 1) * blk] for rg in range(n_groups)]
               for n in range(RG_BLOCKS)]

    def x_branch(xb, n):
        return _dot(xb, win_ref[:, width + n * blk:width + (n + 1) * blk])

    def gate_block(xb, n, zx):
        cols = slice(n * blk, (n + 1) * blk)
        xc = _causal_conv(zx, hist_refs, n * blk, [w[:, cols] for w in cw], cb[:, cols], batch)
        gts = _dot(xc.astype(BF16), wg_ref[n])
        zx_next = x_branch(xb, n + 1) if n + 1 < RG_BLOCKS else None
        zg = _dot(xb, win_ref[:, cols])
        r = jax.nn.sigmoid(gts[:, :blk] + b_r[:, cols])
        ig = jax.nn.sigmoid(gts[:, blk:] + b_i[:, cols])
        neg_log_a = r * c_softplus[:, cols]
        a = jnp.exp(-neg_log_a)
        th = jnp.tanh(neg_log_a)
        v = th / (1.0 + th)
        root = jnp.where(v > 0.0, v * lax.rsqrt(v), 0.0)
        bb = (math.sqrt(2.0) * root) * (ig * xc)

        def step(h, row):
            h = a[row:row + SUBLANES, :] * h + bb[row:row + SUBLANES, :]
            return h, (h,)

        h_state[n], (hs,) = _slab_scan(step, h_state[n], sub_steps, batch)
        return (hs * jax.nn.gelu(zg)).astype(BF16), zx_next

    def finish(s, x, hg):
        y = _dot(hg, wout_ref[...])
        out_ref[s * sub_rows:(s + 1) * sub_rows, :] = _layer_norm(alpha * x + y, ln_g, ln_b)

    pending = None
    for s in range(sg.steps // sub_steps):
        x = x_ref[s * sub_rows:(s + 1) * sub_rows, :]
        xb = x.astype(BF16)
        blocks = []
        zx = x_branch(xb, 0)
        for n in range(RG_BLOCKS):
            hg, zx = gate_block(xb, n, zx)
            blocks.append(hg)
            if n == 0 and pending is not None:
                finish(*pending)
        pending = (s, x, jnp.concatenate(blocks, axis=1))
    finish(*pending)
    for n in range(RG_BLOCKS):
        for rg in range(n_groups):
            hn_ref[rg * SUBLANES:(rg + 1) * SUBLANES, n * blk:(n + 1) * blk] = h_state[n][rg]


def _rg_layer(hs, dims, inits, p, j, ln_row, alpha):
    d_model = hs[0].shape[-1]
    width = p["rg_w_out"].shape[1]
    n_taps = p["rg_conv_w"].shape[1]
    segs = _segments(hs, dims, inits, n_taps - 1, False, sub_rows=RG_SUB_ROWS)
    weights = [
        _pick(p["rg_w_in"], {0: j}), _whole(p["rg_conv_w"]), _whole(p["rg_conv_b"]),
        _pick(p["rg_w_gates"], {0: j}), _whole(p["rg_b_gates"]), _whole(p["rg_lam"]),
        _pick(p["rg_w_out"], {0: j}), _whole(p["ln_g"]), _whole(p["ln_b"])]
    assert len(weights) == RG_WEIGHTS
    n_carry = n_taps
    kern = functools.partial(_rg_kernel, layer=j, ln_row=ln_row, alpha=alpha, segs=segs, n_carry=n_carry)
    flat = _call(
        kern, "rglru_mixer_ln", segs, hs, inits, weights,
        [[_row_shape(sg, d_model, False)] + [jax.ShapeDtypeStruct((sg.batch, width), F32)] * n_carry
         for sg in segs],
        [[_row_spec(sg, d_model, False)] + [_carry_spec((sg.batch, width))] * n_carry for sg in segs])
    return _per_segment(flat, segs, 1 + n_carry)


FFN_WEIGHTS = 6


def _ffn_kernel(*refs, layer, ln_row, alpha, segs, n_hist):
    xs, inits, weights, outs, scratch = _split_refs(refs, segs, FFN_WEIGHTS, 1 + n_hist)
    for x_ref, init_refs, out_refs, act_scr, sg in zip(xs, inits, outs, scratch, segs):
        _ffn_segment(x_ref, init_refs, weights, out_refs, act_scr, sg, layer, ln_row, alpha)


def _ffn_segment(x_ref, init_refs, weights, out_refs, act_scr, sg, layer, ln_row, alpha):
    wup_ref, cw_ref, cb_ref, wd_ref, g_ref, b_ref = weights
    out_ref, hist_refs = out_refs[0], out_refs[1:]
    batch, sub_steps = sg.batch, sg.sub_steps
    ff = wd_ref.shape[0]
    n_taps = cw_ref.shape[1]
    sub_rows = sub_steps * batch
    _init_carry(hist_refs, init_refs)
    ln_g, ln_b = g_ref[ln_row[0], ln_row[1]:ln_row[1] + 1, :], b_ref[ln_row[0], ln_row[1]:ln_row[1] + 1, :]

    def finish(s, x):
        half_steps = sub_steps // 2 if sub_steps % 2 == 0 else sub_steps
        for t0 in range(0, sub_steps, half_steps):
            rows = slice(t0 * batch, (t0 + half_steps) * batch)
            y = _dot(act_scr[s % 2, rows, :], wd_ref[...])
            _store_rows(out_ref, s * sub_steps + t0, half_steps, batch, sg.out_batch_major,
                        _layer_norm(alpha * x[rows] + y, ln_g, ln_b))

    pending = None
    for s in range(sg.steps // sub_steps):
        x = x_ref[s * sub_rows:(s + 1) * sub_rows, :]
        xb = x.astype(BF16)
        for j in range(ff // FF_TILE):
            val, gate = [
                _causal_conv(_dot(xb, wup_ref[:, col:col + FF_TILE]), hist_refs, col,
                             [cw_ref[layer, k:k + 1, col:col + FF_TILE] for k in range(n_taps)],
                             cb_ref[layer:layer + 1, col:col + FF_TILE], batch)
                for col in (j * FF_TILE, ff + j * FF_TILE)]
            act_scr[s % 2, :, j * FF_TILE:(j + 1) * FF_TILE] = (jax.nn.gelu(gate) * val).astype(BF16)
            if j == 0 and pending is not None:
                finish(*pending)
        pending = (s, x)
    finish(*pending)


def _ffn_layer(hs, dims, inits, p, i, ln_row, alpha, out_batch_major):
    d_model = hs[0].shape[-1]
    ff = p["ffn_w_down"].shape[1]
    n_taps = p["ffn_conv_w"].shape[1]
    segs = _segments(hs, dims, inits, n_taps - 1, out_batch_major)
    weights = [
        _pick(p["ffn_w_up"], {0: i}), _whole(p["ffn_conv_w"]), _whole(p["ffn_conv_b"]),
        _pick(p["ffn_w_down"], {0: i}), _whole(p["ln_g"]), _whole(p["ln_b"])]
    assert len(weights) == FFN_WEIGHTS
    kern = functools.partial(_ffn_kernel, layer=i, ln_row=ln_row, alpha=alpha, segs=segs, n_hist=n_taps - 1)
    flat = _call(
        kern, "convffn_ln", segs, hs, inits, weights,
        [[_row_shape(sg, d_model, out_batch_major)]
         + [jax.ShapeDtypeStruct((sg.batch, 2 * ff), F32)] * (n_taps - 1) for sg in segs],
        [[_row_spec(sg, d_model, out_batch_major)] + [_carry_spec((sg.batch, 2 * ff))] * (n_taps - 1)
         for sg in segs],
        scratch=[pltpu.VMEM((2, sg.sub_steps * sg.batch, ff), BF16) for sg in segs])
    return _per_segment(flat, segs, n_taps)


def _trunks(xs, inits, p, alpha):
    dims = [x.shape[:2] for x in xs]
    depth = p["ffn_w_up"].shape[0]
    news = [dict(s5_re=[], s5_im=[], rg_h=[], rg_conv=[], ffn_conv=[]) for _ in xs]
    hs = list(xs)
    for i in range(depth):
        j = i // 2
        if i % 2 == 0:
            st = [() if init is None else (init["s5_re"][j], init["s5_im"][j]) for init in inits]
            outs = _s5_layer(hs, dims, st, p, j, (i, 0), alpha)
            for new, (_, sr, si) in zip(news, outs):
                new["s5_re"].append(sr)
                new["s5_im"].append(si)
        else:
            st = [() if init is None else (init["rg_h"][j], *init["rg_conv"][j]) for init in inits]
            outs = _rg_layer(hs, dims, st, p, j, (i, 0), alpha)
            for new, (_, hn, *hist) in zip(news, outs):
                new["rg_h"].append(hn)
                new["rg_conv"].append(hist)
        hs = [o[0] for o in outs]
        st = [() if init is None else tuple(init["ffn_conv"][i]) for init in inits]
        outs = _ffn_layer(hs, dims, st, p, i, (i, 1), alpha, out_batch_major=(i == depth - 1))
        for new, (_, *hist) in zip(news, outs):
            new["ffn_conv"].append(hist)
        hs = [o[0] for o in outs]
    return hs, news


def _carried(new):
    one = lambda a: _whole(a)
    return dict(s5_re=[one(a) for a in new["s5_re"]], s5_im=[one(a) for a in new["s5_im"]],
                rg_h=[one(a) for a in new["rg_h"]],
                rg_conv=[[one(a) for a in hist] for hist in new["rg_conv"]],
                ffn_conv=[[one(a) for a in hist] for hist in new["ffn_conv"]])


def _given(s5_re, s5_im, rg_h, rg_conv, ffn_conv):
    flat = lambda a: [_whole(a[k].reshape(a.shape[1], -1)) for k in range(a.shape[0])]

    def steps(a):
        t = a.transpose(0, 2, 1, 3)
        return [[_pick(t, {0: n, 1: k}) for k in range(t.shape[1])] for n in range(t.shape[0])]

    return dict(s5_re=flat(s5_re), s5_im=flat(s5_im), rg_h=[_pick(rg_h, {0: n}) for n in range(rg_h.shape[0])],
                rg_conv=steps(rg_conv), ffn_conv=steps(ffn_conv))


def _stack_states(new, s5_shape):
    s5 = lambda k: jnp.stack(new[k]).reshape((len(new[k]), new[k][0].shape[0]) + s5_shape)
    conv = lambda k: jnp.stack([jnp.stack(hist, axis=1) for hist in new[k]])
    return (s5("s5_re"), s5("s5_im"), jnp.stack(new["rg_h"]), conv("rg_conv"), conv("ffn_conv"))


def kernel(x_prompt, x_sample, state_s5_re, state_s5_im, state_rg_h, state_rg_conv, state_ffn_conv,
           meta_tokens, s5_w_in, s5_lam_re, s5_lam_im, s5_log_step, s5_b_re, s5_b_im, s5_c_re,
           s5_c_im, s5_d, s5_w_out, rg_w_in, rg_conv_w, rg_conv_b, rg_w_gates, rg_b_gates, rg_lam,
           rg_w_out, ffn_w_up, ffn_conv_w, ffn_conv_b, ffn_w_down, ln_g, ln_b):
    depth = ffn_w_up.shape[0]
    alpha = (2 * depth) ** 0.25
    bp = x_prompt.shape[0]
    dt = x_prompt.dtype

    p = dict(
        s5_w_in=s5_w_in.astype(BF16), s5_w_out=s5_w_out.astype(BF16), s5_d=s5_d,
        rg_w_in=rg_w_in.astype(BF16), rg_conv_w=rg_conv_w, rg_conv_b=rg_conv_b,
        rg_w_gates=rg_w_gates.astype(BF16), rg_b_gates=rg_b_gates, rg_lam=rg_lam,
        rg_w_out=rg_w_out.astype(BF16),
        ffn_w_up=ffn_w_up.astype(BF16), ffn_conv_w=ffn_conv_w, ffn_conv_b=ffn_conv_b,
        ffn_w_down=ffn_w_down.astype(BF16), ln_g=ln_g, ln_b=ln_b,
        **_s5_params(s5_lam_re, s5_lam_im, s5_log_step, s5_b_re, s5_b_im, s5_c_re, s5_c_im))
    s5_shape = state_s5_re.shape[2:]

    x_meta = jnp.broadcast_to(meta_tokens[None].astype(dt), (bp,) + meta_tokens.shape)
    given = _given(state_s5_re, state_s5_im, state_rg_h, state_rg_conv, state_ffn_conv)
    (_, y_sample), (meta_states, new_s) = _trunks([x_meta, x_sample], [None, given], p, alpha)
    (y_prompt,), (new_p,) = _trunks([x_prompt], [_carried(meta_states)], p, alpha)
    return (y_prompt, y_sample) + _stack_states(new_p, s5_shape) + _stack_states(new_s, s5_shape)
```

```python
import functools
import math
from typing import NamedTuple

import jax
import jax.numpy as jnp
import numpy as np
from jax import lax
from jax.experimental import pallas as pl
from jax.experimental.pallas import tpu as pltpu

RG_BLOCKS = 4
RG_C = 8.0
LN_EPS = 1e-5

SUBLANES = 8
S5_CHUNK_GROUPS = 8
MXU_COLS = 256
FF_TILE = MXU_COLS
BLOCK_ROWS = 1024
SUB_ROWS = 512
RG_SUB_ROWS = BLOCK_ROWS
VMEM_LIMIT_BYTES = 56 * 1024 * 1024

BF16 = jnp.bfloat16
F32 = jnp.float32


class _Seg(NamedTuple):
    batch: int
    seq_len: int
    steps: int
    sub_steps: int
    n_init: int
    x_batch_major: bool
    out_batch_major: bool


def _dot(a, b):
    return jnp.dot(a, b, preferred_element_type=F32)


def _layer_norm(z, g, b):
    mu = jnp.mean(z, axis=-1, keepdims=True)
    zc = z - mu
    var = jnp.mean(zc * zc, axis=-1, keepdims=True)
    return zc * lax.rsqrt(var + LN_EPS) * g + b


def _largest_divisor(n, limit):
    return max(d for d in range(1, n + 1) if n % d == 0 and d <= limit)


def _segments(hs, dims, inits, min_steps, out_batch_major, sub_rows=SUB_ROWS):
    segs = []
    for h, (batch, seq_len), init in zip(hs, dims, inits):
        steps = _largest_divisor(seq_len, max(BLOCK_ROWS // batch, min_steps))
        sub = _largest_divisor(steps, max(sub_rows // batch, min_steps))
        assert sub >= min_steps and (len(hs) == 1 or steps == seq_len), (seq_len, batch, min_steps)
        segs.append(_Seg(batch, seq_len, steps, sub, len(init), h.ndim == 3, out_batch_major))
    return tuple(segs)


def _whole(a):
    return a, pl.BlockSpec(a.shape, lambda i: (0,) * a.ndim, pipeline_mode=pl.Buffered(1))


def _pick(a, fixed):
    block = tuple(None if d in fixed else n for d, n in enumerate(a.shape))
    index = tuple(fixed.get(d, 0) for d in range(a.ndim))
    return a, pl.BlockSpec(block, lambda i: index, pipeline_mode=pl.Buffered(1))


def _row_spec(sg, d_model, batch_major):
    if batch_major:
        return pl.BlockSpec((sg.batch, sg.steps, d_model), lambda i: (0, i, 0))
    return pl.BlockSpec((sg.steps * sg.batch, d_model), lambda i: (i, 0))


def _row_shape(sg, d_model, batch_major):
    shape = (sg.batch, sg.seq_len, d_model) if batch_major else (sg.seq_len * sg.batch, d_model)
    return jax.ShapeDtypeStruct(shape, F32)


def _carry_spec(shape):
    return pl.BlockSpec(shape, lambda i: (0,) * len(shape))


def _load_rows(x_ref, t0, n_steps, batch, batch_major, order=None):
    if batch_major:
        return jnp.concatenate([x_ref[:, t0 + t, :] for t in (order or range(n_steps))], axis=0)
    if order is None:
        return x_ref[t0 * batch:(t0 + n_steps) * batch, :]
    return jnp.concatenate([x_ref[(t0 + t) * batch:(t0 + t + 1) * batch, :] for t in order], axis=0)


def _store_rows(out_ref, t0, n_steps, batch, batch_major, val, order=None):
    if batch_major or order is not None:
        for i, t in enumerate(order or range(n_steps)):
            if batch_major:
                out_ref[:, t0 + t, :] = val[i * batch:(i + 1) * batch, :]
            else:
                out_ref[(t0 + t) * batch:(t0 + t + 1) * batch, :] = val[i * batch:(i + 1) * batch, :]
    else:
        out_ref[t0 * batch:(t0 + n_steps) * batch, :] = val


def _init_carry(carry_refs, init_refs):
    @pl.when(pl.program_id(0) == 0)
    def _():
        for k, ref in enumerate(carry_refs):
            ref[...] = init_refs[k][...] if init_refs else jnp.zeros_like(ref)


def _slab_scan(step_fn, carry, n_steps, batch):
    n_groups = batch // SUBLANES
    outs = [[None] * n_groups for _ in range(n_steps)]
    carry = list(carry)
    for t in range(n_steps):
        for rg in range(n_groups):
            row = t * batch + rg * SUBLANES
            carry[rg], outs[t][rg] = step_fn(carry[rg], row)
    n_out = len(outs[0][0])
    stacked = [jnp.concatenate([outs[t][rg][k] for t in range(n_steps) for rg in range(n_groups)], axis=0)
               for k in range(n_out)]
    return carry, stacked


def _causal_conv(up, hist_refs, col, cw, cb, batch):
    rows, width = up.shape
    n_hist = len(hist_refs)
    ext = jnp.concatenate([r[:, col:col + width] for r in hist_refs] + [up], axis=0)
    for k in range(n_hist):
        hist_refs[k][:, col:col + width] = ext[rows + k * batch:rows + (k + 1) * batch]
    y = cb + cw[0] * ext[0:rows]
    for k in range(1, n_hist + 1):
        y = y + cw[k] * ext[k * batch:k * batch + rows]
    return y


def _split_refs(refs, segs, n_weights, n_outs):
    xs, pos = refs[:len(segs)], len(segs)
    inits = []
    for sg in segs:
        inits.append(refs[pos:pos + sg.n_init])
        pos += sg.n_init
    weights, pos = refs[pos:pos + n_weights], pos + n_weights
    outs = []
    for _ in segs:
        outs.append(refs[pos:pos + n_outs])
        pos += n_outs
    return xs, inits, weights, outs, refs[pos:]


def _call(kern, name, segs, hs, inits, weights, out_shapes, out_specs, scratch=()):
    d_model = hs[0].shape[-1]
    operands = [pair for init in inits for pair in init] + list(weights)
    arrays, specs = zip(*operands)
    n_blocks = {sg.seq_len // sg.steps for sg in segs}
    assert len(n_blocks) == 1, segs
    return pl.pallas_call(
        kern,
        grid=(n_blocks.pop(),),
        in_specs=[_row_spec(sg, d_model, sg.x_batch_major) for sg in segs] + list(specs),
        out_specs=[spec for per_seg in out_specs for spec in per_seg],
        out_shape=[shape for per_seg in out_shapes for shape in per_seg],
        scratch_shapes=list(scratch),
        compiler_params=pltpu.CompilerParams(
            dimension_semantics=("arbitrary",), vmem_limit_bytes=VMEM_LIMIT_BYTES,
            allow_input_fusion=[False] * len(hs) + [a.dtype == BF16 for a in arrays]),
        name=name,
    )(*hs, *arrays)


def _per_segment(flat, segs, n_outs):
    return [flat[k * n_outs:(k + 1) * n_outs] for k in range(len(segs))]


S5_WEIGHTS = 9


def _s5_kernel(*refs, layer, ln_row, alpha, segs):
    xs, inits, weights, outs, _ = _split_refs(refs, segs, S5_WEIGHTS, 3)
    for x_ref, init_refs, out_refs, sg in zip(xs, inits, outs, segs):
        _s5_segment(x_ref, init_refs, weights, out_refs, sg, layer, ln_row, alpha)


def _s5_segment(x_ref, init_refs, weights, out_refs, sg, layer, ln_row, alpha):
    win_ref, drive_ref, read_ref, a2re_ref, a2im_ref, d_ref, wout_ref, g_ref, b_ref = weights
    out_ref, sre_ref, sim_ref = out_refs
    batch, sub_steps = sg.batch, sg.sub_steps
    d_model = out_ref.shape[1]
    n_chunks, cin2, cst2 = drive_ref.shape
    cin = cin2 // 2
    cst = cst2 // 2
    n_pairs = sub_steps // 2
    pair_rows = n_pairs * batch
    n_groups = batch // SUBLANES
    order = list(range(0, sub_steps, 2)) + list(range(1, sub_steps, 2))
    _init_carry((sre_ref, sim_ref), init_refs)
    ln_g, ln_b = g_ref[ln_row[0], ln_row[1]:ln_row[1] + 1, :], b_ref[ln_row[0], ln_row[1]:ln_row[1] + 1, :]
    d_skip = d_ref[layer:layer + 1, :]

    def rows_of(ref, rg, c):
        return ref[rg * SUBLANES:(rg + 1) * SUBLANES, c * cst:(c + 1) * cst]

    state = [[(rows_of(sre_ref, rg, c), rows_of(sim_ref, rg, c)) for rg in range(n_groups)]
             for c in range(n_chunks)]

    def pair_inputs(u, c):
        cols = slice(c * cin, (c + 1) * cin)
        return jnp.concatenate([u[:pair_rows, cols], u[pair_rows:, cols]], axis=1).astype(BF16)

    def scan_readout(up, drv, c):
        a_re = jnp.broadcast_to(a2re_ref[c:c + 1, :], (SUBLANES, cst))
        a_im = jnp.broadcast_to(a2im_ref[c:c + 1, :], (SUBLANES, cst))

        def step(carry, row):
            s_re, s_im = carry
            n_re = a_re * s_re - a_im * s_im + drv[row:row + SUBLANES, 0:cst]
            n_im = a_re * s_im + a_im * s_re + drv[row:row + SUBLANES, cst:cst2]
            return (n_re, n_im), (s_re, s_im)

        state[c], (st_re, st_im) = _slab_scan(step, state[c], n_pairs, batch)
        lhs = jnp.concatenate([st_re.astype(BF16), st_im.astype(BF16), up], axis=1)
        return _dot(lhs, read_ref[c])

    def finish(s, x, u, ys):
        for parity in range(2):
            rows = slice(parity * pair_rows, (parity + 1) * pair_rows)
            y = jnp.concatenate([yp[:, parity * cin:(parity + 1) * cin] for yp in ys], axis=1) + d_skip * u[rows]
            vg = _dot(jax.nn.gelu(y).astype(BF16), wout_ref[...])
            mix = vg[:, :d_model] * jax.nn.sigmoid(vg[:, d_model:])
            _store_rows(out_ref, s * sub_steps, n_pairs, batch, False,
                        _layer_norm(alpha * x[rows] + mix, ln_g, ln_b),
                        order[parity * n_pairs:(parity + 1) * n_pairs])

    pending = None
    for s in range(sg.steps // sub_steps):
        x = _load_rows(x_ref, s * sub_steps, sub_steps, batch, sg.x_batch_major, order)
        u = _dot(x.astype(BF16), win_ref[...])
        ys = []
        up_next = pair_inputs(u, 0)
        drv_next = _dot(up_next, drive_ref[0])
        for c in range(n_chunks):
            up, drv = up_next, drv_next
            if c + 1 < n_chunks:
                up_next = pair_inputs(u, c + 1)
                drv_next = _dot(up_next, drive_ref[c + 1])
            ys.append(scan_readout(up, drv, c))
            if c == 0 and pending is not None:
                finish(*pending)
        pending = (s, x, u, ys)
    finish(*pending)
    for c in range(n_chunks):
        for rg in range(n_groups):
            sre_ref[rg * SUBLANES:(rg + 1) * SUBLANES, c * cst:(c + 1) * cst] = state[c][rg][0]
            sim_ref[rg * SUBLANES:(rg + 1) * SUBLANES, c * cst:(c + 1) * cst] = state[c][rg][1]


def _s5_layer(hs, dims, inits, p, j, ln_row, alpha):
    d_model = hs[0].shape[-1]
    segs = _segments(hs, dims, inits, 2, False)
    assert all(sg.sub_steps % 2 == 0 for sg in segs), segs
    n_state = p["a2_re"].shape[1] * p["a2_re"].shape[2]
    weights = [
        _pick(p["s5_w_in"], {0: j}), _pick(p["s5_drive"], {0: j}), _pick(p["s5_read"], {0: j}),
        _pick(p["a2_re"], {0: j}), _pick(p["a2_im"], {0: j}), _whole(p["s5_d"]),
        _pick(p["s5_w_out"], {0: j}), _whole(p["ln_g"]), _whole(p["ln_b"])]
    assert len(weights) == S5_WEIGHTS
    kern = functools.partial(_s5_kernel, layer=j, ln_row=ln_row, alpha=alpha, segs=segs)
    state = lambda sg: jax.ShapeDtypeStruct((sg.batch, n_state), F32)
    flat = _call(
        kern, "s5_mixer_ln", segs, hs, inits, weights,
        [[_row_shape(sg, d_model, False), state(sg), state(sg)] for sg in segs],
        [[_row_spec(sg, d_model, False)] + [_carry_spec((sg.batch, n_state))] * 2 for sg in segs])
    return _per_segment(flat, segs, 3)


def _s5_params(lam_re, lam_im, log_step, b_re, b_im, c_re, c_im):
    n_layers, n_groups, n_state, gsz = b_re.shape
    gc = S5_CHUNK_GROUPS
    nc = n_groups // gc
    exact = lax.Precision.HIGHEST
    step = jnp.exp(log_step.astype(F32))[..., None]
    lr, li = lam_re.astype(F32), lam_im.astype(F32)
    mag = jnp.exp(lr * step)
    ab_re, ab_im = mag * jnp.cos(li * step), mag * jnp.sin(li * step)
    den = lr * lr + li * li
    nr, ni = ab_re - 1.0, ab_im
    q_re = ((nr * lr + ni * li) / den)[..., None]
    q_im = ((ni * lr - nr * li) / den)[..., None]
    br, bi = b_re.astype(F32), b_im.astype(F32)
    bb_re = q_re * br - q_im * bi
    bb_im = q_re * bi + q_im * br
    a2_re, a2_im = ab_re * ab_re - ab_im * ab_im, 2.0 * ab_re * ab_im
    lb_re = ab_re[..., None] * bb_re - ab_im[..., None] * bb_im
    lb_im = ab_re[..., None] * bb_im + ab_im[..., None] * bb_re
    cr, ci = c_re.astype(F32), c_im.astype(F32)

    def c_times(m_re, m_im):
        m_re, m_im = m_re[:, :, None, :], m_im[:, :, None, :]
        return cr * m_re - ci * m_im, cr * m_im + ci * m_re

    def re_c_times(m_re, m_im):
        return (jnp.einsum("ngop,ngpi->ngio", cr, m_re, precision=exact)
                - jnp.einsum("ngop,ngpi->ngio", ci, m_im, precision=exact))

    def compact(parts, row_axis, col_axis):
        t = jnp.stack([jnp.stack(row) for row in parts])
        t = t.reshape(2, 2, n_layers, nc, gc, t.shape[4], t.shape[5])
        t = t.transpose(2, 3, 0, 4, 5 + row_axis, 1, 5 + col_axis)
        return t.reshape(n_layers, nc, 2 * gc * t.shape[4], 2 * t.shape[6])

    def expand(x, group_rows, group_cols):
        c = x.shape[3] // 2
        tile = np.kron(np.eye(2), np.kron(np.ones((1, gc)), np.eye(c)))
        keep = group_rows[:, None] == group_cols[None, :]
        t = jnp.einsum("ncrj,jq->ncrq", x.astype(BF16), jnp.asarray(tile, BF16),
                       preferred_element_type=F32)
        return jnp.where(jnp.asarray(keep), t, 0.0).astype(BF16)

    group_of = lambda n, per_group: (np.arange(n) // per_group) % gc
    drive = expand(compact([[lb_re, lb_im], [bb_re, bb_im]], 1, 0),
                   group_of(2 * gc * gsz, gsz), group_of(2 * gc * n_state, n_state))
    cl_re, cl_im = c_times(ab_re, ab_im)
    cl2_re, cl2_im = c_times(a2_re, a2_im)
    k0 = re_c_times(bb_re, bb_im)
    k1 = re_c_times(lb_re, lb_im)
    read_rows = jnp.concatenate([compact([[cl_re, cl2_re], [-cl_im, -cl2_im]], 1, 0),
                                 compact([[k0, k1], [jnp.zeros_like(k0), k0]], 0, 1)], axis=2)
    read = expand(read_rows,
                  np.concatenate([group_of(2 * gc * n_state, n_state), group_of(2 * gc * gsz, gsz)]),
                  group_of(2 * gc * gsz, gsz))
    return dict(s5_drive=drive, s5_read=read, a2_re=a2_re.reshape(n_layers, nc, gc * n_state),
                a2_im=a2_im.reshape(n_layers, nc, gc * n_state))


RG_WEIGHTS = 9


def _softplus(z):
    return jnp.maximum(z, 0.0) + jnp.log1p(jnp.exp(-jnp.abs(z)))


def _rg_kernel(*refs, layer, ln_row, alpha, segs, n_carry):
    xs, inits, weights, outs, _ = _split_refs(refs, segs, RG_WEIGHTS, 1 + n_carry)
    for x_ref, init_refs, out_refs, sg in zip(xs, inits, outs, segs):
        _rg_segment(x_ref, init_refs, weights, out_refs, sg, layer, ln_row, alpha)


def _rg_segment(x_ref, init_refs, weights, out_refs, sg, layer, ln_row, alpha):
    win_ref, cw_ref, cb_ref, wg_ref, bg_ref, lam_ref, wout_ref, g_ref, b_ref = weights
    out_ref, hn_ref, hist_refs = out_refs[0], out_refs[1], out_refs[2:]
    batch, sub_steps = sg.batch, sg.sub_steps
    width = hn_ref.shape[1]
    blk = width // RG_BLOCKS
    sub_rows = sub_steps * batch
    n_groups = batch // SUBLANES
    _init_carry((hn_ref, *hist_refs), init_refs)
    ln_g, ln_b = g_ref[ln_row[0], ln_row[1]:ln_row[1] + 1, :], b_ref[ln_row[0], ln_row[1]:ln_row[1] + 1, :]
    cw = [cw_ref[layer, k:k + 1, :] for k in range(cw_ref.shape[1])]
    cb = cb_ref[layer:layer + 1, :]
    b_r, b_i = bg_ref[layer:layer + 1, :width], bg_ref[layer:layer + 1, width:]

    c_softplus = RG_C * _softplus(-lam_ref[layer:layer + 1, :])
    h_state = [[hn_ref[rg * SUBLANES:(rg + 1) * SUBLANES, n * blk:(n + 1) * blk] for rg in range(n_groups)]
               for n in range(RG_BLOCKS)]

    def x_branch(xb, n):
        return _dot(xb, win_ref[:, width + n * blk:width + (n + 1) * blk])

    def gate_block(xb, n, zx):
        cols = slice(n * blk, (n + 1) * blk)
        xc = _causal_conv(zx, hist_refs, n * blk, [w[:, cols] for w in cw], cb[:, cols], batch)
        gts = _dot(xc.astype(BF16), wg_ref[n])
        zx_next = x_branch(xb, n + 1) if n + 1 < RG_BLOCKS else None
        zg = _dot(xb, win_ref[:, cols])
        r = jax.nn.sigmoid(gts[:, :blk] + b_r[:, cols])
        ig = jax.nn.sigmoid(gts[:, blk:] + b_i[:, cols])
        neg_log_a = r * c_softplus[:, cols]
        a = jnp.exp(-neg_log_a)
        th = jnp.tanh(neg_log_a)
        v = th / (1.0 + th)
        root = jnp.where(v > 0.0, v * lax.rsqrt(v), 0.0)
        bb = (math.sqrt(2.0) * root) * (ig * xc)

        def step(h, row):
            h = a[row:row + SUBLANES, :] * h + bb[row:row + SUBLANES, :]
            return h, (h,)

        h_state[n], (hs,) = _slab_scan(step, h_state[n], sub_steps, batch)
        return (hs * jax.nn.gelu(zg)).astype(BF16), zx_next

    def finish(s, x, hg):
        y = _dot(hg, wout_ref[...])
        out_ref[s * sub_rows:(s + 1) * sub_rows, :] = _layer_norm(alpha * x + y, ln_g, ln_b)

    pending = None
    for s in range(sg.steps // sub_steps):
        x = x_ref[s * sub_rows:(s + 1) * sub_rows, :]
        xb = x.astype(BF16)
        blocks = []
        zx = x_branch(xb, 0)
        for n in range(RG_BLOCKS):
            hg, zx = gate_block(xb, n, zx)
            blocks.append(hg)
            if n == 0 and pending is not None:
                finish(*pending)
        pending = (s, x, jnp.concatenate(blocks, axis=1))
    finish(*pending)
    for n in range(RG_BLOCKS):
        for rg in range(n_groups):
            hn_ref[rg * SUBLANES:(rg + 1) * SUBLANES, n * blk:(n + 1) * blk] = h_state[n][rg]


def _rg_layer(hs, dims, inits, p, j, ln_row, alpha):
    d_model = hs[0].shape[-1]
    width = p["rg_w_out"].shape[1]
    n_taps = p["rg_conv_w"].shape[1]
    segs = _segments(hs, dims, inits, n_taps - 1, False, sub_rows=RG_SUB_ROWS)
    weights = [
        _pick(p["rg_w_in"], {0: j}), _whole(p["rg_conv_w"]), _whole(p["rg_conv_b"]),
        _pick(p["rg_w_gates"], {0: j}), _whole(p["rg_b_gates"]), _whole(p["rg_lam"]),
        _pick(p["rg_w_out"], {0: j}), _whole(p["ln_g"]), _whole(p["ln_b"])]
    assert len(weights) == RG_WEIGHTS
    n_carry = n_taps
    kern = functools.partial(_rg_kernel, layer=j, ln_row=ln_row, alpha=alpha, segs=segs, n_carry=n_carry)
    flat = _call(
        kern, "rglru_mixer_ln", segs, hs, inits, weights,
        [[_row_shape(sg, d_model, False)] + [jax.ShapeDtypeStruct((sg.batch, width), F32)] * n_carry
         for sg in segs],
        [[_row_spec(sg, d_model, False)] + [_carry_spec((sg.batch, width))] * n_carry for sg in segs])
    return _per_segment(flat, segs, 1 + n_carry)


FFN_WEIGHTS = 6


def _ffn_kernel(*refs, layer, ln_row, alpha, segs, n_hist):
    xs, inits, weights, outs, scratch = _split_refs(refs, segs, FFN_WEIGHTS, 1 + n_hist)
    for x_ref, init_refs, out_refs, act_scr, sg in zip(xs, inits, outs, scratch, segs):
        _ffn_segment(x_ref, init_refs, weights, out_refs, act_scr, sg, layer, ln_row, alpha)


def _ffn_segment(x_ref, init_refs, weights, out_refs, act_scr, sg, layer, ln_row, alpha):
    wup_ref, cw_ref, cb_ref, wd_ref, g_ref, b_ref = weights
    out_ref, hist_refs = out_refs[0], out_refs[1:]
    batch, sub_steps = sg.batch, sg.sub_steps
    ff = wd_ref.shape[0]
    n_taps = cw_ref.shape[1]
    sub_rows = sub_steps * batch
    _init_carry(hist_refs, init_refs)
    ln_g, ln_b = g_ref[ln_row[0], ln_row[1]:ln_row[1] + 1, :], b_ref[ln_row[0], ln_row[1]:ln_row[1] + 1, :]

    def finish(s, x):
        half_steps = sub_steps // 2 if sub_steps % 2 == 0 else sub_steps
        for t0 in range(0, sub_steps, half_steps):
            rows = slice(t0 * batch, (t0 + half_steps) * batch)
            y = _dot(act_scr[s % 2, rows, :], wd_ref[...])
            _store_rows(out_ref, s * sub_steps + t0, half_steps, batch, sg.out_batch_major,
                        _layer_norm(alpha * x[rows] + y, ln_g, ln_b))

    pending = None
    for s in range(sg.steps // sub_steps):
        x = x_ref[s * sub_rows:(s + 1) * sub_rows, :]
        xb = x.astype(BF16)
        for j in range(ff // FF_TILE):
            val, gate = [
                _causal_conv(_dot(xb, wup_ref[:, col:col + FF_TILE]), hist_refs, col,
                             [cw_ref[layer, k:k + 1, col:col + FF_TILE] for k in range(n_taps)],
                             cb_ref[layer:layer + 1, col:col + FF_TILE], batch)
                for col in (j * FF_TILE, ff + j * FF_TILE)]
            act_scr[s % 2, :, j * FF_TILE:(j + 1) * FF_TILE] = (jax.nn.gelu(gate) * val).astype(BF16)
            if j == 0 and pending is not None:
                finish(*pending)
        pending = (s, x)
    finish(*pending)


def _ffn_layer(hs, dims, inits, p, i, ln_row, alpha, out_batch_major):
    d_model = hs[0].shape[-1]
    ff = p["ffn_w_down"].shape[1]
    n_taps = p["ffn_conv_w"].shape[1]
    segs = _segments(hs, dims, inits, n_taps - 1, out_batch_major)
    weights = [
        _pick(p["ffn_w_up"], {0: i}), _whole(p["ffn_conv_w"]), _whole(p["ffn_conv_b"]),
        _pick(p["ffn_w_down"], {0: i}), _whole(p["ln_g"]), _whole(p["ln_b"])]
    assert len(weights) == FFN_WEIGHTS
    kern = functools.partial(_ffn_kernel, layer=i, ln_row=ln_row, alpha=alpha, segs=segs, n_hist=n_taps - 1)
    flat = _call(
        kern, "convffn_ln", segs, hs, inits, weights,
        [[_row_shape(sg, d_model, out_batch_major)]
         + [jax.ShapeDtypeStruct((sg.batch, 2 * ff), F32)] * (n_taps - 1) for sg in segs],
        [[_row_spec(sg, d_model, out_batch_major)] + [_carry_spec((sg.batch, 2 * ff))] * (n_taps - 1)
         for sg in segs],
        scratch=[pltpu.VMEM((2, sg.sub_steps * sg.batch, ff), BF16) for sg in segs])
    return _per_segment(flat, segs, n_taps)


def _trunks(xs, inits, p, alpha):
    dims = [x.shape[:2] for x in xs]
    depth = p["ffn_w_up"].shape[0]
    news = [dict(s5_re=[], s5_im=[], rg_h=[], rg_conv=[], ffn_conv=[]) for _ in xs]
    hs = list(xs)
    for i in range(depth):
        j = i // 2
        if i % 2 == 0:
            st = [() if init is None else (init["s5_re"][j], init["s5_im"][j]) for init in inits]
            outs = _s5_layer(hs, dims, st, p, j, (i, 0), alpha)
            for new, (_, sr, si) in zip(news, outs):
                new["s5_re"].append(sr)
                new["s5_im"].append(si)
        else:
            st = [() if init is None else (init["rg_h"][j], *init["rg_conv"][j]) for init in inits]
            outs = _rg_layer(hs, dims, st, p, j, (i, 0), alpha)
            for new, (_, hn, *hist) in zip(news, outs):
                new["rg_h"].append(hn)
                new["rg_conv"].append(hist)
        hs = [o[0] for o in outs]
        st = [() if init is None else tuple(init["ffn_conv"][i]) for init in inits]
        outs = _ffn_layer(hs, dims, st, p, i, (i, 1), alpha, out_batch_major=(i == depth - 1))
        for new, (_, *hist) in zip(news, outs):
            new["ffn_conv"].append(hist)
        hs = [o[0] for o in outs]
    return hs, news


def _carried(new):
    one = lambda a: _whole(a)
    return dict(s5_re=[one(a) for a in new["s5_re"]], s5_im=[one(a) for a in new["s5_im"]],
                rg_h=[one(a) for a in new["rg_h"]],
                rg_conv=[[one(a) for a in hist] for hist in new["rg_conv"]],
                ffn_conv=[[one(a) for a in hist] for hist in new["ffn_conv"]])


def _given(s5_re, s5_im, rg_h, rg_conv, ffn_conv):
    flat = lambda a: [_whole(a[k].reshape(a.shape[1], -1)) for k in range(a.shape[0])]

    def steps(a):
        t = a.transpose(0, 2, 1, 3)
        return [[_pick(t, {0: n, 1: k}) for k in range(t.shape[1])] for n in range(t.shape[0])]

    return dict(s5_re=flat(s5_re), s5_im=flat(s5_im), rg_h=[_pick(rg_h, {0: n}) for n in range(rg_h.shape[0])],
                rg_conv=steps(rg_conv), ffn_conv=steps(ffn_conv))


def _stack_states(new, s5_shape):
    s5 = lambda k: jnp.stack(new[k]).reshape((len(new[k]), new[k][0].shape[0]) + s5_shape)
    conv = lambda k: jnp.stack([jnp.stack(hist, axis=1) for hist in new[k]])
    return (s5("s5_re"), s5("s5_im"), jnp.stack(new["rg_h"]), conv("rg_conv"), conv("ffn_conv"))


def kernel(x_prompt, x_sample, state_s5_re, state_s5_im, state_rg_h, state_rg_conv, state_ffn_conv,
           meta_tokens, s5_w_in, s5_lam_re, s5_lam_im, s5_log_step, s5_b_re, s5_b_im, s5_c_re,
           s5_c_im, s5_d, s5_w_out, rg_w_in, rg_conv_w, rg_conv_b, rg_w_gates, rg_b_gates, rg_lam,
           rg_w_out, ffn_w_up, ffn_conv_w, ffn_conv_b, ffn_w_down, ln_g, ln_b):
    depth = ffn_w_up.shape[0]
    alpha = (2 * depth) ** 0.25
    bp = x_prompt.shape[0]
    dt = x_prompt.dtype

    p = dict(
        s5_w_in=s5_w_in.astype(BF16), s5_w_out=s5_w_out.astype(BF16), s5_d=s5_d,
        rg_w_in=rg_w_in.astype(BF16), rg_conv_w=rg_conv_w, rg_conv_b=rg_conv_b,
        rg_w_gates=rg_w_gates.astype(BF16), rg_b_gates=rg_b_gates, rg_lam=rg_lam,
        rg_w_out=rg_w_out.astype(BF16),
        ffn_w_up=ffn_w_up.astype(BF16), ffn_conv_w=ffn_conv_w, ffn_conv_b=ffn_conv_b,
        ffn_w_down=ffn_w_down.astype(BF16), ln_g=ln_g, ln_b=ln_b,
        **_s5_params(s5_lam_re, s5_lam_im, s5_log_step, s5_b_re, s5_b_im, s5_c_re, s5_c_im))
    s5_shape = state_s5_re.shape[2:]

    x_meta = jnp.broadcast_to(meta_tokens[None].astype(dt), (bp,) + meta_tokens.shape)
    given = _given(state_s5_re, state_s5_im, state_rg_h, state_rg_conv, state_ffn_conv)
    (_, y_sample), (meta_states, new_s) = _trunks([x_meta, x_sample], [None, given], p, alpha)
    (y_prompt,), (new_p,) = _trunks([x_prompt], [_carried(meta_states)], p, alpha)
    return (y_prompt, y_sample) + _stack_states(new_p, s5_shape) + _stack_states(new_s, s5_shape)
```
